```python
import jax, jax.numpy as jnp
from jax import lax
import numpy as np

D_MODEL = 2048
BATCH = 16
SEQ = 2048
DEPTH = 2

HEAD_DIM = 128
ATTN_HEADS = 8
ATTN_WIDTH = ATTN_HEADS * HEAD_DIM
CONV_GROUPS = 8
CONV_WIDTH = CONV_GROUPS * HEAD_DIM
CONV_LEN = 3
ROPE_THETA = 500000.0
ROPE_DIM = HEAD_DIM // 4
MOBA_BLOCK = 256
MOBA_TOPK = 3
Q_CHUNK = 128
N_EXPERTS = 64
MOE_TOP_K = 8
N_GROUPS = 8
TOPK_GROUPS = 4
EXPERT_DIM = 512
SHARED_DIM = 512
ROUTED_SCALE = 2.5
EXPERT_BLOCK = 128
NORM_EPS = 1e-6
IN_WIDTH = 3 * ATTN_WIDTH + 3 * CONV_WIDTH + 2 * D_MODEL

kernel_name = "hybrid_moba_shortconv_moe_adaln"


def rms_norm(x, g):
    x32 = x.astype(jnp.float32)
    y = x32 * lax.rsqrt(jnp.mean(x32 * x32, axis=-1, keepdims=True) + NORM_EPS)
    return (y * g.astype(jnp.float32)).astype(x.dtype)


def modulate(h, shift, scale):
    return h * (1 + scale) + shift


def rope_tables(positions):
    inv_freq = ROPE_THETA ** (-jnp.arange(0, ROPE_DIM, 2, dtype=jnp.float32) / ROPE_DIM)
    ang = positions.astype(jnp.float32)[..., None] * inv_freq
    return jnp.cos(ang)[:, :, None, :], jnp.sin(ang)[:, :, None, :]


def apply_partial_rope(x, cos, sin):
    half = ROPE_DIM // 2
    x1 = x[..., :half].astype(jnp.float32)
    x2 = x[..., half:ROPE_DIM].astype(jnp.float32)
    rot = jnp.concatenate([x1 * cos - x2 * sin, x2 * cos + x1 * sin], axis=-1).astype(x.dtype)
    return jnp.concatenate([rot, x[..., ROPE_DIM:]], axis=-1)


def moba_attention(q, k, v):
    B, S, H, Dh = q.shape
    nb = -(-S // MOBA_BLOCK)
    pad = nb * MOBA_BLOCK - S
    n_sel = min(MOBA_TOPK, nb - 1)
    nc = S // Q_CHUNK
    scale = HEAD_DIM ** -0.5
    padw = ((0, 0), (0, pad), (0, 0), (0, 0))
    kh = jnp.pad(k, padw).reshape(B, nb, MOBA_BLOCK, H, Dh).transpose(0, 3, 1, 2, 4)
    vh = jnp.pad(v, padw).reshape(B, nb, MOBA_BLOCK, H, Dh).transpose(0, 3, 1, 2, 4)
    xs = (q.reshape(B * nc, Q_CHUNK, H, Dh),
          jnp.arange(B * nc, dtype=jnp.int32) // nc,
          jnp.arange(B * nc, dtype=jnp.int32) % nc)
    if n_sel > 0:
        k_mean = kh.mean(axis=3)
        gate = jnp.einsum('bshd,bhnd->bshn', q, k_mean).astype(jnp.float32)
        q_blk = jnp.arange(S) // MOBA_BLOCK
        past = jnp.arange(nb)[None, :] < q_blk[:, None]
        gate = jnp.where(past[None, :, None, :], gate, -jnp.inf)
        sel = lax.top_k(gate, n_sel)[1]
        xs = xs + (sel.reshape(B * nc, Q_CHUNK, H, n_sel),)
    heads = jnp.arange(H)[None, :, None]

    def step(args):
        q_i, b, ci = args[0], args[1], args[2]
        q_i = q_i * scale
        kb = kh[b]
        vb = vh[b]
        own = (ci * Q_CHUNK) // MOBA_BLOCK
        k_own = lax.dynamic_index_in_dim(kb, own, axis=1, keepdims=False)
        v_own = lax.dynamic_index_in_dim(vb, own, axis=1, keepdims=False)
        q_pos = ci * Q_CHUNK + jnp.arange(Q_CHUNK)
        key_pos = own * MOBA_BLOCK + jnp.arange(MOBA_BLOCK)
        s_own = jnp.einsum('qhd,hkd->qhk', q_i, k_own).astype(jnp.float32)
        s_own = jnp.where((key_pos[None, :] <= q_pos[:, None])[:, None, :], s_own, -jnp.inf)
        if n_sel > 0:
            sel_i = args[3]
            k_sel = kb[heads, sel_i]
            v_sel = vb[heads, sel_i]
            s_sel = jnp.einsum('qhd,qhmkd->qhmk', q_i, k_sel).astype(jnp.float32)
            valid = jnp.arange(n_sel) < own
            s_sel = jnp.where(valid[None, None, :, None], s_sel, -jnp.inf)
            s = jnp.concatenate([s_sel.reshape(Q_CHUNK, H, n_sel * MOBA_BLOCK), s_own], axis=-1)
            p = jax.nn.softmax(s, axis=-1).astype(v.dtype)
            p_sel = p[..., :n_sel * MOBA_BLOCK].reshape(Q_CHUNK, H, n_sel, MOBA_BLOCK)
            p_own = p[..., n_sel * MOBA_BLOCK:]
            return (jnp.einsum('qhmk,qhmkd->qhd', p_sel, v_sel)
                    + jnp.einsum('qhk,hkd->qhd', p_own, v_own))
        p_own = jax.nn.softmax(s_own, axis=-1).astype(v.dtype)
        return jnp.einsum('qhk,hkd->qhd', p_own, v_own)

    o = lax.map(step, xs)
    return o.reshape(B, S, H * Dh)


def causal_depthwise_conv(u, w):
    C = u.shape[-1]
    return lax.conv_general_dilated(
        u, w.astype(u.dtype)[:, None, :], window_strides=(1,),
        padding=[(CONV_LEN - 1, 0)], dimension_numbers=('NWC', 'WIO', 'NWC'),
        feature_group_count=C)


def hybrid_mixer(h, cos, sin, w_in, conv_w, w_branch, w_out):
    B, S, _ = h.shape
    A, Cw = ATTN_WIDTH, CONV_WIDTH
    proj = h @ w_in
    splits = [A, 2 * A, 3 * A, 3 * A + Cw, 3 * A + 2 * Cw, 3 * A + 3 * Cw, 3 * A + 3 * Cw + D_MODEL]
    q, k, v, cb, cc, cx, ga, gc = jnp.split(proj, splits, axis=-1)
    q = apply_partial_rope(q.reshape(B, S, ATTN_HEADS, HEAD_DIM), cos, sin)
    k = apply_partial_rope(k.reshape(B, S, ATTN_HEADS, HEAD_DIM), cos, sin)
    v = v.reshape(B, S, ATTN_HEADS, HEAD_DIM)
    y_attn = moba_attention(q, k, v)
    y_conv = cb * causal_depthwise_conv(cc * cx, conv_w)
    merged = (jax.nn.sigmoid(ga) * (y_attn @ w_branch[0])
              + jax.nn.sigmoid(gc) * (y_conv @ w_branch[1]))
    return merged @ w_out


def route(h, w_router, router_bias):
    T = h.shape[0]
    scores = jax.nn.sigmoid(h.astype(jnp.float32) @ w_router.astype(jnp.float32))
    biased = scores + router_bias.astype(jnp.float32)
    per_group = N_EXPERTS // N_GROUPS
    grp_score = lax.top_k(biased.reshape(T, N_GROUPS, per_group), 2)[0].sum(-1)
    grp_idx = lax.top_k(grp_score, TOPK_GROUPS)[1]
    grp_mask = (grp_idx[..., None] == jnp.arange(N_GROUPS)).any(axis=-2)
    masked = jnp.where(jnp.repeat(grp_mask, per_group, axis=-1), biased, -jnp.inf)
    idx = lax.top_k(masked, MOE_TOP_K)[1]
    wts = jnp.take_along_axis(scores, idx, axis=-1)
    wts = wts / jnp.sum(wts, axis=-1, keepdims=True) * ROUTED_SCALE
    return idx, wts


def routed_experts(h, idx, wts, w_gate, w_up, w_down):
    T, D = h.shape
    A = T * MOE_TOP_K
    n_blocks = -(-(A + N_EXPERTS * (EXPERT_BLOCK - 1)) // EXPERT_BLOCK)
    R = n_blocks * EXPERT_BLOCK
    flat_e = idx.reshape(A)
    order = jnp.argsort(flat_e)
    sorted_e = flat_e[order]
    sizes = jnp.bincount(flat_e, length=N_EXPERTS)
    starts = jnp.cumsum(sizes) - sizes
    padded = (sizes + EXPERT_BLOCK - 1) // EXPERT_BLOCK * EXPERT_BLOCK
    pad_ends = jnp.cumsum(padded)
    pad_starts = pad_ends - padded
    rows = pad_starts[sorted_e] + jnp.arange(A) - starts[sorted_e]
    tok_rows = jnp.zeros((R,), jnp.int32).at[rows].set((order // MOE_TOP_K).astype(jnp.int32))
    wt_rows = jnp.zeros((R,), h.dtype).at[rows].set(wts.reshape(A)[order].astype(h.dtype))
    block_e = jnp.minimum(
        jnp.searchsorted(pad_ends, jnp.arange(n_blocks) * EXPERT_BLOCK, side='right'),
        N_EXPERTS - 1)

    def expert_block(args):
        toks, wt, e = args
        xb = h[toks]
        yb = (jax.nn.silu(xb @ w_gate[e]) * (xb @ w_up[e])) @ w_down[e]
        return yb * wt[:, None]

    y = lax.map(expert_block, (tok_rows.reshape(n_blocks, EXPERT_BLOCK),
                               wt_rows.reshape(n_blocks, EXPERT_BLOCK), block_e))
    return jax.ops.segment_sum(y.reshape(R, D), tok_rows, num_segments=T)


def moe_ffn(h, w_router, router_bias, w_gate, w_up, w_down, ws_gate, ws_up, ws_down):
    idx, wts = route(h, w_router, router_bias)
    shared = (jax.nn.silu(h @ ws_gate) * (h @ ws_up)) @ ws_down
    return shared + routed_experts(h, idx, wts, w_gate, w_up, w_down)


def setup_inputs(seed: int = 0) -> dict:
    key = jax.random.key(seed)
    ks = jax.random.split(key, 20)
    f32 = jnp.float32
    nrm = lambda k, shape, s: jax.random.normal(k, shape, f32) * s
    D = D_MODEL
    offsets = jax.random.randint(ks[2], (BATCH, 1), 0, 4096, dtype=jnp.int32)
    return {
        "x": nrm(ks[0], (BATCH, SEQ, D), 1.0),
        "c": nrm(ks[1], (BATCH, D), 1.0),
        "positions": offsets + jnp.arange(SEQ, dtype=jnp.int32)[None, :],
        "norm_mix": 1.0 + nrm(ks[3], (DEPTH, D), 0.02),
        "w_ada": nrm(ks[4], (DEPTH, D, 6 * D), 0.5 * D ** -0.5),
        "b_ada": nrm(ks[5], (DEPTH, 6 * D), 0.02),
        "w_in": nrm(ks[6], (DEPTH, D, IN_WIDTH), D ** -0.5),
        "conv_w": nrm(ks[7], (DEPTH, CONV_LEN, CONV_WIDTH), CONV_LEN ** -0.5),
        "w_branch": nrm(ks[8], (DEPTH, 2, ATTN_WIDTH, D), ATTN_WIDTH ** -0.5),
        "w_out": nrm(ks[9], (DEPTH, D, D), D ** -0.5),
        "norm_ffn": 1.0 + nrm(ks[10], (DEPTH, D), 0.02),
        "w_router": nrm(ks[11], (DEPTH, D, N_EXPERTS), D ** -0.5),
        "router_bias": nrm(ks[12], (DEPTH, N_EXPERTS), 0.01),
        "w_gate": nrm(ks[13], (DEPTH, N_EXPERTS, D, EXPERT_DIM), D ** -0.5),
        "w_up": nrm(ks[14], (DEPTH, N_EXPERTS, D, EXPERT_DIM), D ** -0.5),
        "w_down": nrm(ks[15], (DEPTH, N_EXPERTS, EXPERT_DIM, D), EXPERT_DIM ** -0.5),
        "ws_gate": nrm(ks[16], (DEPTH, D, SHARED_DIM), D ** -0.5),
        "ws_up": nrm(ks[17], (DEPTH, D, SHARED_DIM), D ** -0.5),
        "ws_down": nrm(ks[18], (DEPTH, SHARED_DIM, D), SHARED_DIM ** -0.5),
        "final_norm": 1.0 + nrm(ks[19], (D,), 0.02),
    }


def reference(x, c, positions, norm_mix, w_ada, b_ada, w_in, conv_w, w_branch, w_out,
              norm_ffn, w_router, router_bias, w_gate, w_up, w_down,
              ws_gate, ws_up, ws_down, final_norm):
    B, S, D = x.shape
    cos, sin = rope_tables(positions)
    c_act = jax.nn.silu(c)
    for l in range(DEPTH):
        ada = c_act @ w_ada[l] + b_ada[l]
        sh_m, sc_m, g_m, sh_f, sc_f, g_f = jnp.split(ada[:, None, :], 6, axis=-1)
        h = modulate(rms_norm(x, norm_mix[l]), sh_m, sc_m)
        x = x + g_m * hybrid_mixer(h, cos, sin, w_in[l], conv_w[l], w_branch[l], w_out[l])
        h = modulate(rms_norm(x, norm_ffn[l]), sh_f, sc_f)
        y = moe_ffn(h.reshape(B * S, D), w_router[l], router_bias[l], w_gate[l], w_up[l],
                    w_down[l], ws_gate[l], ws_up[l], ws_down[l])
        x = x + g_f * y.reshape(B, S, D)
    return rms_norm(x, final_norm)
```

```python
import functools

import jax
import jax.numpy as jnp
from jax import lax
from jax.experimental import pallas as pl
from jax.experimental.pallas import tpu as pltpu

F32 = jnp.float32
BF16 = jnp.bfloat16
U32 = jnp.uint32
I32 = jnp.int32

HEAD_DIM = 128
ROPE_DIM = HEAD_DIM // 4
ROPE_THETA = 500000.0
MOBA_BLOCK = 256
MOBA_TOPK = 3
CONV_LEN = 3
N_EXPERTS = 64
MOE_TOP_K = 8
N_GROUPS = 8
GROUP_SIZE = N_EXPERTS // N_GROUPS
TOPK_GROUPS = 4
ROUTED_SCALE = 2.5
NORM_EPS = 1e-6

VMEM_LIMIT_BYTES = 56 * 1024 * 1024
SUBLANES = 8
EXPERT_ROWS = 256
NEG_BIG = -1e30

NT_DIMS = (((1,), (1,)), ((), ()))


def _cparams(n_axes):
    return pltpu.CompilerParams(
        dimension_semantics=("arbitrary",) * n_axes,
        vmem_limit_bytes=VMEM_LIMIT_BYTES)


def _bits(v):
    return lax.bitcast_convert_type(v, U32)


def _pack_halves(v):
    n = v.shape[1] // 2
    lo = _bits(v[:, :n].astype(BF16).astype(F32))
    hi = _bits(v[:, n:].astype(BF16).astype(F32))
    return hi | (lo >> 16)


def _unpack_lo(w):
    return lax.bitcast_convert_type(w << 16, F32)


def _unpack_hi(w):
    return lax.bitcast_convert_type(w & jnp.uint32(0xFFFF0000), F32)


LANES = 128


def _load_rows(ref, n, ch):
    return jnp.concatenate([ref[pl.ds(s, n, stride=ch), :] for s in range(ch)], axis=1)


def _store_rows(ref, val, n, ch):
    for s in range(ch):
        ref[pl.ds(s, n, stride=ch), :] = val[:, s * LANES:(s + 1) * LANES]


def _silu(v):
    return v * jax.nn.sigmoid(v)


def _rms_modulate(x, g, shift, scale):
    y = x * lax.rsqrt(jnp.mean(x * x, axis=-1, keepdims=True) + NORM_EPS)
    return (y * g) * (1.0 + scale) + shift


def _ada_kernel(c_ref, w_ref, b_ref, o_ref):
    ca = _silu(c_ref[...]).astype(BF16)
    o_ref[0] = jnp.dot(ca, w_ref[0].astype(BF16), preferred_element_type=F32) + b_ref[0]


def _ada_call(c, w_ada, b_ada):
    L, D, N = w_ada.shape
    B = c.shape[0]
    tn = min(512, N)
    return pl.pallas_call(
        _ada_kernel,
        grid=(L, N // tn),
        in_specs=[
            pl.BlockSpec((B, D), lambda l, j: (0, 0)),
            pl.BlockSpec((1, D, tn), lambda l, j: (l, 0, j)),
            pl.BlockSpec((1, 1, tn), lambda l, j: (l, 0, j)),
        ],
        out_specs=pl.BlockSpec((1, B, tn), lambda l, j: (l, 0, j)),
        out_shape=jax.ShapeDtypeStruct((L, B, N), F32),
        compiler_params=_cparams(2),
        name="ada",
    )(c, w_ada, b_ada.reshape(L, 1, N))


def _inproj_kernel(x_ref, g_ref, ada_ref, w_ref, o_ref, h_sc):
    @pl.when(pl.program_id(1) == 0)
    def _():
        h = _rms_modulate(x_ref[...], g_ref[...], ada_ref[0:1, :], ada_ref[1:2, :])
        h_sc[...] = h.astype(BF16)

    o_ref[...] = jnp.dot(h_sc[...], w_ref[...], preferred_element_type=F32).astype(o_ref.dtype)


def _inproj_call(x, g, ada_l, w, S):
    T, D = x.shape
    N = w.shape[1]
    tm = min(1024, S)
    tn = min(512, N)
    return pl.pallas_call(
        _inproj_kernel,
        grid=(T // tm, N // tn),
        in_specs=[
            pl.BlockSpec((tm, D), lambda i, j: (i, 0)),
            pl.BlockSpec((1, D), lambda i, j: (0, 0)),
            pl.BlockSpec((None, 6, D), lambda i, j: ((i * tm) // S, 0, 0)),
            pl.BlockSpec((D, tn), lambda i, j: (0, j)),
        ],
        out_specs=pl.BlockSpec((tm, tn), lambda i, j: (i, j)),
        out_shape=jax.ShapeDtypeStruct((T, N), BF16),
        scratch_shapes=[pltpu.VMEM((tm, D), BF16)],
        compiler_params=_cparams(2),
        name="inproj",
    )(x, g.reshape(1, D), ada_l, w)


def _rope(v, c, s):
    half = ROPE_DIM // 2
    lane = lax.broadcasted_iota(I32, v.shape, 1)
    partner = jnp.where(lane < half,
                        pltpu.roll(v, HEAD_DIM - half, axis=1),
                        pltpu.roll(v, half, axis=1))
    return v * c + partner * s


def _attn_kernel(q_ref, k_ref, v_ref, cq_ref, sq_ref, ck_ref, sk_ref, o_ref,
                 kr_sc, vt_sc, km_sc, sel_sc, m_sc, l_sc, acc_sc, *, nb):
    qi = pl.program_id(2)
    blk = MOBA_BLOCK

    @pl.when(qi == 0)
    def _():
        km_sc[...] = jnp.zeros_like(km_sc)
        for j in range(nb):
            rows = pl.ds(j * blk, blk)
            kr = _rope(k_ref[rows, :].astype(F32), ck_ref[rows, :], sk_ref[rows, :])
            kr_sc[j] = kr.astype(BF16)
            km_sc[j:j + 1, :] = jnp.mean(kr, axis=0, keepdims=True)
            vt_sc[j] = v_ref[rows, :].astype(F32).T.astype(BF16)

    q = _rope(q_ref[...].astype(F32), cq_ref[...], sq_ref[...])
    gt = lax.dot_general(km_sc[...], q, NT_DIMS, precision=lax.Precision.HIGHEST,
                         preferred_element_type=F32)
    row = lax.broadcasted_iota(I32, gt.shape, 0)
    past = row < qi
    gm = jnp.where(past, gt, -jnp.inf)
    rank = jnp.zeros(gt.shape, F32)
    for jp in range(nb):
        gj = gm[jp:jp + 1, :]
        beats = (gj > gm) | ((gj == gm) & (jp < row))
        rank = rank + beats.astype(F32)
    sel_sc[...] = (past & (rank < MOBA_TOPK)).astype(F32)

    qs = (q * (HEAD_DIM ** -0.5)).astype(BF16)

    st = lax.dot_general(kr_sc[qi], qs, NT_DIMS, preferred_element_type=F32)
    kidx = lax.broadcasted_iota(I32, st.shape, 0)
    qidx = lax.broadcasted_iota(I32, st.shape, 1)
    st = jnp.where(kidx <= qidx, st, NEG_BIG)
    m0 = jnp.max(st, axis=0, keepdims=True)
    p0 = jnp.exp(st - m0)
    m_sc[...] = m0
    l_sc[...] = jnp.sum(p0, axis=0, keepdims=True)
    acc_sc[...] = jnp.dot(vt_sc[qi], p0.astype(BF16), preferred_element_type=F32)

    for j in range(nb - 1):
        @pl.when(j < qi)
        def _(j=j):
            sj = lax.dot_general(kr_sc[j], qs, NT_DIMS, preferred_element_type=F32)
            sj = jnp.where(sel_sc[j:j + 1, :] > 0.0, sj, NEG_BIG)
            m_old = m_sc[...]
            m_new = jnp.maximum(m_old, jnp.max(sj, axis=0, keepdims=True))
            alpha = jnp.exp(m_old - m_new)
            p = jnp.exp(sj - m_new)
            l_sc[...] = alpha * l_sc[...] + jnp.sum(p, axis=0, keepdims=True)
            acc_sc[...] = alpha * acc_sc[...] + jnp.dot(vt_sc[j], p.astype(BF16),
                                                        preferred_element_type=F32)
            m_sc[...] = m_new

    o = acc_sc[...] / l_sc[...]
    o_ref[...] = o.T.astype(o_ref.dtype)


def _attn_call(proj, rope_c, rope_s, B, S, H):
    T = B * S
    blk = MOBA_BLOCK
    nb = S // blk
    nbp = -(-nb // SUBLANES) * SUBLANES
    dh = HEAD_DIM
    kern = functools.partial(_attn_kernel, nb=nb)
    return pl.pallas_call(
        kern,
        grid=(B, H, nb),
        in_specs=[
            pl.BlockSpec((blk, dh), lambda b, h, i: (b * nb + i, h)),
            pl.BlockSpec((S, dh), lambda b, h, i: (b, H + h)),
            pl.BlockSpec((S, dh), lambda b, h, i: (b, 2 * H + h)),
            pl.BlockSpec((blk, dh), lambda b, h, i: (b * nb + i, 0)),
            pl.BlockSpec((blk, dh), lambda b, h, i: (b * nb + i, 0)),
            pl.BlockSpec((S, dh), lambda b, h, i: (b, 0)),
            pl.BlockSpec((S, dh), lambda b, h, i: (b, 0)),
        ],
        out_specs=pl.BlockSpec((blk, dh), lambda b, h, i: (b * nb + i, h)),
        out_shape=jax.ShapeDtypeStruct((T, H * dh), BF16),
        scratch_shapes=[
            pltpu.VMEM((nb, blk, dh), BF16),
            pltpu.VMEM((nb, dh, blk), BF16),
            pltpu.VMEM((nbp, dh), F32),
            pltpu.VMEM((nbp, blk), F32),
            pltpu.VMEM((1, blk), F32),
            pltpu.VMEM((1, blk), F32),
            pltpu.VMEM((dh, blk), F32),
        ],
        compiler_params=_cparams(3),
        name="moba_attn",
    )(proj, proj, proj, rope_c, rope_s, rope_c, rope_s)


def _mix1_kernel(ya_ref, cb_ref, cc_ref, cx_ref, hcc_ref, hcx_ref, ga_ref, gc_ref, cw_ref,
                 wb0_ref, wb1_ref, o_ref, *, tm, S):
    i = pl.program_id(0)
    u = cc_ref[...].astype(F32) * cx_ref[...].astype(F32)
    hu = hcc_ref[...].astype(F32) * hcx_ref[...].astype(F32)
    seq_start = (i * tm) % S == 0
    hu = jnp.where(seq_start, 0.0, hu)
    h1 = hu[SUBLANES - 1:SUBLANES, :]
    h2 = hu[SUBLANES - 2:SUBLANES - 1, :]
    r = lax.broadcasted_iota(I32, u.shape, 0)
    u1 = jnp.where(r == 0, h1, pltpu.roll(u, 1, axis=0))
    u2 = jnp.where(r == 0, h2, jnp.where(r == 1, h1, pltpu.roll(u, 2, axis=0)))
    conv = cw_ref[0:1, :] * u2 + cw_ref[1:2, :] * u1 + cw_ref[2:3, :] * u
    yc = (cb_ref[...].astype(F32) * conv).astype(BF16)
    a = jnp.dot(ya_ref[...], wb0_ref[...], preferred_element_type=F32)
    c = jnp.dot(yc, wb1_ref[...], preferred_element_type=F32)
    merged = (jax.nn.sigmoid(ga_ref[...].astype(F32)) * a
              + jax.nn.sigmoid(gc_ref[...].astype(F32)) * c)
    o_ref[...] = merged.astype(o_ref.dtype)


def _mix1_call(ya, proj, conv_w, wb0, wb1, S, D):
    T, A = ya.shape
    assert conv_w.shape == (CONV_LEN, A) and (6 * A) % D == 0
    tm = min(256, S)
    gcol = 6 * A // D
    hb = tm // SUBLANES
    kern = functools.partial(_mix1_kernel, tm=tm, S=S)
    halo = lambda col: pl.BlockSpec((SUBLANES, A), lambda i: (jnp.maximum(i * hb - 1, 0), col))
    return pl.pallas_call(
        kern,
        grid=(T // tm,),
        in_specs=[
            pl.BlockSpec((tm, A), lambda i: (i, 0)),
            pl.BlockSpec((tm, A), lambda i: (i, 3)),
            pl.BlockSpec((tm, A), lambda i: (i, 4)),
            pl.BlockSpec((tm, A), lambda i: (i, 5)),
            halo(4),
            halo(5),
            pl.BlockSpec((tm, D), lambda i: (i, gcol)),
            pl.BlockSpec((tm, D), lambda i: (i, gcol + 1)),
            pl.BlockSpec((CONV_LEN, A), lambda i: (0, 0)),
            pl.BlockSpec((A, D), lambda i: (0, 0)),
            pl.BlockSpec((A, D), lambda i: (0, 0)),
        ],
        out_specs=pl.BlockSpec((tm, D), lambda i: (i, 0)),
        out_shape=jax.ShapeDtypeStruct((T, D), BF16),
        compiler_params=_cparams(1),
        name="mix_merge",
    )(ya, proj, proj, proj, proj, proj, proj, proj, conv_w, wb0, wb1)


def _mix2_kernel(m_ref, wo_ref, x_ref, ada_ref, g_ref, wrt_ref, xo_ref, hp_ref, lg_ref):
    out = jnp.dot(m_ref[...], wo_ref[...], preferred_element_type=F32)
    x = x_ref[...] + ada_ref[2:3, :] * out
    xo_ref[...] = x
    h = _rms_modulate(x, g_ref[...], ada_ref[3:4, :], ada_ref[4:5, :])
    tm, d = x.shape
    _store_rows(hp_ref, _pack_halves(h), tm, d // 2 // LANES)
    lg_ref[...] = lax.dot_general(wrt_ref[...], h, NT_DIMS, precision=lax.Precision.HIGHEST,
                                  preferred_element_type=F32)


def _mix2_call(merged, w_out, x, ada_l, g, wrt, S):
    T, D = x.shape
    tm = min(256, S)
    E = wrt.shape[0]
    ch = D // 2 // LANES
    return pl.pallas_call(
        _mix2_kernel,
        grid=(T // tm,),
        in_specs=[
            pl.BlockSpec((tm, D), lambda i: (i, 0)),
            pl.BlockSpec((D, D), lambda i: (0, 0)),
            pl.BlockSpec((tm, D), lambda i: (i, 0)),
            pl.BlockSpec((None, 6, D), lambda i: ((i * tm) // S, 0, 0)),
            pl.BlockSpec((1, D), lambda i: (0, 0)),
            pl.BlockSpec((E, D), lambda i: (0, 0)),
        ],
        out_specs=[
            pl.BlockSpec((tm, D), lambda i: (i, 0)),
            pl.BlockSpec((tm * ch, LANES), lambda i: (i, 0)),
            pl.BlockSpec((E, tm), lambda i: (0, i)),
        ],
        out_shape=[
            jax.ShapeDtypeStruct((T, D), F32),
            jax.ShapeDtypeStruct((T * ch, LANES), U32),
            jax.ShapeDtypeStruct((E, T), F32),
        ],
        compiler_params=_cparams(1),
        name="outproj_norm_router",
    )(merged, w_out, x, ada_l, g.reshape(1, D), wrt)


def _route_kernel(lg_ref, bias_ref, tri_ref, idx_ref, wt_ref, rs_ref, cnt_ref, carry_sc):
    step = pl.program_id(0)

    @pl.when(step == 0)
    def _():
        carry_sc[...] = jnp.zeros_like(carry_sc)

    G = N_GROUPS
    scores = jax.nn.sigmoid(lg_ref[...])
    biased = scores + bias_ref[...]
    sc = [scores[i * G:(i + 1) * G, :] for i in range(GROUP_SIZE)]
    bi = [biased[i * G:(i + 1) * G, :] for i in range(GROUP_SIZE)]
    grow = lax.broadcasted_iota(I32, bi[0].shape, 0)
    eid = [grow * GROUP_SIZE + i for i in range(GROUP_SIZE)]
    neg_inf = jnp.float32(-jnp.inf)

    m1 = functools.reduce(jnp.maximum, bi)
    first = jnp.full(grow.shape, GROUP_SIZE, I32)
    for i in reversed(range(GROUP_SIZE)):
        first = jnp.where(bi[i] == m1, i, first)
    m2 = functools.reduce(jnp.maximum,
                          [jnp.where(first == i, neg_inf, bi[i]) for i in range(GROUP_SIZE)])
    gs = m1 + m2

    grank = jnp.zeros(gs.shape, F32)
    for gp in range(G):
        v = gs[gp:gp + 1, :]
        grank = grank + ((v > gs) | ((v == gs) & (gp < grow))).astype(F32)
    gsel = grank < TOPK_GROUPS

    ms = [jnp.where(gsel, b, neg_inf) for b in bi]
    chosen = [jnp.zeros(grow.shape, jnp.bool_) for _ in range(GROUP_SIZE)]
    picked_idx = []
    picked_score = []
    for _ in range(MOE_TOP_K):
        mx = jnp.max(functools.reduce(jnp.maximum, ms), axis=0, keepdims=True)
        cand = functools.reduce(
            jnp.minimum, [jnp.where(ms[i] == mx, eid[i], N_EXPERTS) for i in range(GROUP_SIZE)])
        ce = jnp.min(cand, axis=0, keepdims=True)
        hit = [eid[i] == ce for i in range(GROUP_SIZE)]
        s_k = jnp.sum(functools.reduce(
            jnp.add, [jnp.where(hit[i], sc[i], 0.0) for i in range(GROUP_SIZE)]),
            axis=0, keepdims=True)
        ms = [jnp.where(hit[i], neg_inf, ms[i]) for i in range(GROUP_SIZE)]
        chosen = [chosen[i] | hit[i] for i in range(GROUP_SIZE)]
        picked_idx.append(ce)
        picked_score.append(s_k)

    wsum = functools.reduce(jnp.add, picked_score)
    for k in range(MOE_TOP_K):
        idx_ref[k:k + 1, :] = picked_idx[k]
        wt_ref[k:k + 1, :] = picked_score[k] / wsum * ROUTED_SCALE

    selm = jnp.concatenate([c.astype(F32) for c in chosen], axis=0)
    pre = jnp.dot(selm.astype(BF16), tri_ref[...], preferred_element_type=F32) + carry_sc[:, 0:1]
    pre_slab = [pre[i * G:(i + 1) * G, :] for i in range(GROUP_SIZE)]
    for k in range(MOE_TOP_K):
        rk = jnp.sum(functools.reduce(
            jnp.add, [jnp.where(eid[i] == picked_idx[k], pre_slab[i], 0.0)
                      for i in range(GROUP_SIZE)]), axis=0, keepdims=True)
        rs_ref[k:k + 1, :] = rk.astype(I32)
    carry_sc[...] = carry_sc[...] + jnp.sum(selm, axis=1, keepdims=True)
    cnt_ref[...] = carry_sc[...]


def _route_call(lgt, bias_perm, T):
    E = lgt.shape[0]
    tr = min(512, T)
    tri = (lax.broadcasted_iota(I32, (tr, tr), 0) < lax.broadcasted_iota(I32, (tr, tr), 1)).astype(BF16)
    return pl.pallas_call(
        _route_kernel,
        grid=(T // tr,),
        in_specs=[
            pl.BlockSpec((E, tr), lambda i: (0, i)),
            pl.BlockSpec((E, 1), lambda i: (0, 0)),
            pl.BlockSpec((tr, tr), lambda i: (0, 0)),
        ],
        out_specs=[
            pl.BlockSpec((MOE_TOP_K, tr), lambda i: (0, i)),
            pl.BlockSpec((MOE_TOP_K, tr), lambda i: (0, i)),
            pl.BlockSpec((MOE_TOP_K, tr), lambda i: (0, i)),
            pl.BlockSpec((E, 128), lambda i: (0, 0)),
        ],
        out_shape=[
            jax.ShapeDtypeStruct((MOE_TOP_K, T), I32),
            jax.ShapeDtypeStruct((MOE_TOP_K, T), F32),
            jax.ShapeDtypeStruct((MOE_TOP_K, T), I32),
            jax.ShapeDtypeStruct((E, 128), F32),
        ],
        scratch_shapes=[pltpu.VMEM((E, 128), F32)],
        compiler_params=_cparams(1),
        name="route",
    )(lgt, bias_perm.reshape(E, 1), tri)


def _dispatch_kernel(zs_ref, nu_ref, hp_ref, dest_ref, xs_ref, z_sc, sem, *, td, ch, n_blocks):
    i = pl.program_id(0)
    zrows = EXPERT_ROWS * ch

    @pl.when(i == 0)
    def _():
        z_sc[...] = jnp.zeros_like(z_sc)
        zero_wait = pltpu.make_async_copy(z_sc, xs_ref.at[pl.ds(0, zrows), :], sem)

        def zero_block(b, carry):
            b0 = pl.multiple_of(b * zrows, zrows)
            pltpu.make_async_copy(z_sc, xs_ref.at[pl.ds(b0, zrows), :], sem).start()
            return carry

        def wait_block(b, carry):
            zero_wait.wait()
            return carry

        for e in range(N_EXPERTS):
            @pl.when(zs_ref[e] >= 0)
            def _(e=e):
                zero_block(zs_ref[e], 0)
        lax.fori_loop(nu_ref[0], n_blocks, zero_block, 0)
        for e in range(N_EXPERTS):
            @pl.when(zs_ref[e] >= 0)
            def _():
                zero_wait.wait()
        lax.fori_loop(nu_ref[0], n_blocks, wait_block, 0)

    def issue(t, carry):
        src = hp_ref.at[pl.ds(pl.multiple_of(t * ch, ch), ch), :]
        for k in range(MOE_TOP_K):
            r0 = pl.multiple_of(dest_ref[t * MOE_TOP_K + k] * ch, ch)
            pltpu.make_async_copy(src, xs_ref.at[pl.ds(r0, ch), :], sem).start()
        return carry

    lax.fori_loop(0, td, issue, 0)
    for k in range(MOE_TOP_K):
        pltpu.make_async_copy(hp_ref, xs_ref.at[pl.ds(0, td * ch), :], sem).wait()


def _dispatch_call(hp, dest_flat, zstart, n_used, n_blocks, ch):
    T = hp.shape[0] // ch
    td = min(256, T)
    n_rows = n_blocks * EXPERT_ROWS
    kern = functools.partial(_dispatch_kernel, td=td, ch=ch, n_blocks=n_blocks)
    return pl.pallas_call(
        kern,
        grid_spec=pltpu.PrefetchScalarGridSpec(
            num_scalar_prefetch=2,
            grid=(T // td,),
            in_specs=[
                pl.BlockSpec((td * ch, LANES), lambda i, zs, nu: (i, 0)),
                pl.BlockSpec((td * MOE_TOP_K,), lambda i, zs, nu: (i,), memory_space=pltpu.SMEM),
            ],
            out_specs=pl.BlockSpec(memory_space=pl.ANY),
            scratch_shapes=[pltpu.VMEM((EXPERT_ROWS * ch, LANES), U32), pltpu.SemaphoreType.DMA],
        ),
        out_shape=jax.ShapeDtypeStruct((n_rows * ch, LANES), U32),
        compiler_params=_cparams(1),
        name="dispatch",
    )(zstart, n_used, hp, dest_flat)


def _expert_kernel(be_ref, nu_ref, xs_ref, wg_ref, wu_ref, wd_ref, o_ref, wg_sc, wu_sc, wd_sc):
    i = pl.program_id(0)
    prev = be_ref[jnp.maximum(i - 1, 0)]
    changed = (i == 0) | (be_ref[i] != prev)

    @pl.when(changed)
    def _():
        wg_sc[...] = wg_ref[...].astype(BF16)
        wu_sc[...] = wu_ref[...].astype(BF16)
        wd_sc[...] = wd_ref[...].astype(BF16)

    @pl.when(i < nu_ref[0])
    def _():
        half = wg_sc.shape[0] // 2
        ch = half // LANES
        w = _load_rows(xs_ref, EXPERT_ROWS, ch)
        xlo = _unpack_lo(w).astype(BF16)
        xhi = _unpack_hi(w).astype(BF16)
        gate = (jnp.dot(xlo, wg_sc[:half, :], preferred_element_type=F32)
                + jnp.dot(xhi, wg_sc[half:, :], preferred_element_type=F32))
        up = (jnp.dot(xlo, wu_sc[:half, :], preferred_element_type=F32)
              + jnp.dot(xhi, wu_sc[half:, :], preferred_element_type=F32))
        act = (_silu(gate) * up).astype(BF16)
        out = jnp.dot(act, wd_sc[...], preferred_element_type=F32)
        _store_rows(o_ref, _pack_halves(out), EXPERT_ROWS, ch)

    @pl.when(i >= nu_ref[0])
    def _():
        o_ref[...] = jnp.zeros_like(o_ref)


def _expert_call(xs, block_e, n_used, w_gate, w_up, w_down, n_blocks):
    E, D, F = w_gate.shape
    ch = D // 2 // LANES
    rows = lambda i, be, nu: (jnp.minimum(i, nu[0] - 1), 0)
    out_rows = lambda i, be, nu: (i, 0)
    wmap = lambda i, be, nu: (be[i], 0, 0)
    return pl.pallas_call(
        _expert_kernel,
        grid_spec=pltpu.PrefetchScalarGridSpec(
            num_scalar_prefetch=2,
            grid=(n_blocks,),
            in_specs=[
                pl.BlockSpec((EXPERT_ROWS * ch, LANES), rows),
                pl.BlockSpec((None, D, F), wmap),
                pl.BlockSpec((None, D, F), wmap),
                pl.BlockSpec((None, F, D), wmap),
            ],
            out_specs=pl.BlockSpec((EXPERT_ROWS * ch, LANES), out_rows),
            scratch_shapes=[pltpu.VMEM((D, F), BF16), pltpu.VMEM((D, F), BF16),
                            pltpu.VMEM((F, D), BF16)],
        ),
        out_shape=jax.ShapeDtypeStruct((n_blocks * EXPERT_ROWS * ch, LANES), U32),
        compiler_params=_cparams(1),
        name="experts",
    )(block_e, n_used, xs, w_gate, w_up, w_down)


def _combine_kernel(dest_ref, wt_ref, hp_ref, x_ref, ada_ref, wsg_ref, wsu_ref, wsd_ref, fn_ref,
                    eo_ref, xo_ref, gbuf, sem, *, tc, final):
    half = x_ref.shape[1] // 2
    ch = half // LANES

    def issue(t, carry):
        t0 = pl.multiple_of(t * ch, ch)
        for k in range(MOE_TOP_K):
            r0 = pl.multiple_of(dest_ref[t * MOE_TOP_K + k] * ch, ch)
            pltpu.make_async_copy(eo_ref.at[pl.ds(r0, ch), :], gbuf.at[k, pl.ds(t0, ch), :],
                                  sem).start()
        return carry

    lax.fori_loop(0, tc, issue, 0)

    w = _load_rows(hp_ref, tc, ch)
    hlo = _unpack_lo(w).astype(BF16)
    hhi = _unpack_hi(w).astype(BF16)
    gate = (jnp.dot(hlo, wsg_ref[:half, :], preferred_element_type=F32)
            + jnp.dot(hhi, wsg_ref[half:, :], preferred_element_type=F32))
    up = (jnp.dot(hlo, wsu_ref[:half, :], preferred_element_type=F32)
          + jnp.dot(hhi, wsu_ref[half:, :], preferred_element_type=F32))
    act = (_silu(gate) * up).astype(BF16)
    shared = jnp.dot(act, wsd_ref[...], preferred_element_type=F32)

    for k in range(MOE_TOP_K):
        pltpu.make_async_copy(eo_ref.at[pl.ds(0, tc * ch), :], gbuf.at[k], sem).wait()

    wts = wt_ref[...]
    ylo = shared[:, :half]
    yhi = shared[:, half:]
    for k in range(MOE_TOP_K):
        g = _load_rows(gbuf.at[k], tc, ch)
        wk = wts[:, k:k + 1]
        ylo = ylo + wk * _unpack_lo(g)
        yhi = yhi + wk * _unpack_hi(g)
    gf = ada_ref[5:6, :]
    xlo = x_ref[:, :half] + gf[:, :half] * ylo
    xhi = x_ref[:, half:] + gf[:, half:] * yhi
    if final:
        ss = (jnp.sum(xlo * xlo, axis=-1, keepdims=True)
              + jnp.sum(xhi * xhi, axis=-1, keepdims=True))
        inv = lax.rsqrt(ss / (2 * half) + NORM_EPS)
        xlo = xlo * inv * fn_ref[:, :half]
        xhi = xhi * inv * fn_ref[:, half:]
    xo_ref[:, :half] = xlo
    xo_ref[:, half:] = xhi


def _combine_call(dest_flat, wts, hp, x, ada_l, wsg, wsu, wsd, fnorm, eo, S, final):
    T, D = x.shape
    ch = D // 2 // LANES
    Fs = wsg.shape[1]
    tc = min(128, S)
    kern = functools.partial(_combine_kernel, tc=tc, final=final)
    return pl.pallas_call(
        kern,
        grid=(T // tc,),
        in_specs=[
            pl.BlockSpec((tc * MOE_TOP_K,), lambda i: (i,), memory_space=pltpu.SMEM),
            pl.BlockSpec((tc, MOE_TOP_K), lambda i: (i, 0)),
            pl.BlockSpec((tc * ch, LANES), lambda i: (i, 0)),
            pl.BlockSpec((tc, D), lambda i: (i, 0)),
            pl.BlockSpec((None, 6, D), lambda i: ((i * tc) // S, 0, 0)),
            pl.BlockSpec((D, Fs), lambda i: (0, 0)),
            pl.BlockSpec((D, Fs), lambda i: (0, 0)),
            pl.BlockSpec((Fs, D), lambda i: (0, 0)),
            pl.BlockSpec((1, D), lambda i: (0, 0)),
            pl.BlockSpec(memory_space=pl.ANY),
        ],
        out_specs=pl.BlockSpec((tc, D), lambda i: (i, 0)),
        out_shape=jax.ShapeDtypeStruct((T, D), F32),
        scratch_shapes=[pltpu.VMEM((MOE_TOP_K, tc * ch, LANES), U32), pltpu.SemaphoreType.DMA],
        compiler_params=_cparams(1),
        name="combine_shared",
    )(dest_flat, wts, hp, x, ada_l, wsg, wsu, wsd, fnorm.reshape(1, D), eo)


def _rope_tables(positions):
    half = ROPE_DIM // 2
    inv_freq = ROPE_THETA ** (-jnp.arange(0, ROPE_DIM, 2, dtype=F32) / ROPE_DIM)
    ang = positions.astype(F32).reshape(-1, 1) * inv_freq
    cos, sin = jnp.cos(ang), jnp.sin(ang)
    T = ang.shape[0]
    rest = HEAD_DIM - ROPE_DIM
    c = jnp.concatenate([cos, cos, jnp.ones((T, rest), F32)], axis=-1)
    s = jnp.concatenate([-sin, sin, jnp.zeros((T, rest), F32)], axis=-1)
    return c, s


def _expert_row_perm():
    r = jnp.arange(N_EXPERTS)
    return (r % N_GROUPS) * GROUP_SIZE + r // N_GROUPS


def kernel(x, c, positions, norm_mix, w_ada, b_ada, w_in, conv_w, w_branch, w_out, norm_ffn, w_router, router_bias, w_gate, w_up, w_down, ws_gate, ws_up, ws_down, final_norm):
    B, S, D = x.shape
    L = w_ada.shape[0]
    T = B * S
    A = w_branch.shape[2]
    H = A // HEAD_DIM
    assert S % MOBA_BLOCK == 0 and w_in.shape[2] == 6 * A + 2 * D

    n_assign = T * MOE_TOP_K
    n_blocks = -(-(n_assign + N_EXPERTS * (EXPERT_ROWS - 1)) // EXPERT_ROWS)

    rope_c, rope_s = _rope_tables(positions)
    ada = _ada_call(c, w_ada, b_ada).reshape(L, B, 6, D)
    perm = _expert_row_perm()
    xf = x.reshape(T, D)

    for l in range(L):
        proj = _inproj_call(xf, norm_mix[l], ada[l], w_in[l].astype(BF16), S)
        ya = _attn_call(proj, rope_c, rope_s, B, S, H)
        merged = _mix1_call(ya, proj, conv_w[l], w_branch[l, 0].astype(BF16),
                            w_branch[l, 1].astype(BF16), S, D)
        xf, hp, lgt = _mix2_call(merged, w_out[l].astype(BF16), xf, ada[l], norm_ffn[l],
                                 w_router[l].T[perm], S)

        idx_t, wt_t, rs_t, cnt = _route_call(lgt, router_bias[l][perm], T)
        counts = jnp.zeros((N_EXPERTS,), I32).at[perm].set(cnt[:, 0].astype(I32))
        padded = (counts + EXPERT_ROWS - 1) // EXPERT_ROWS * EXPERT_ROWS
        pad_ends = jnp.cumsum(padded)
        pad_starts = pad_ends - padded
        dest = jnp.take(pad_starts, idx_t) + rs_t
        dest_flat = dest.T.reshape(-1)
        n_used = (pad_ends[-1] // EXPERT_ROWS).astype(I32)
        blk = jnp.minimum(jnp.arange(n_blocks, dtype=I32), n_used - 1)
        block_e = jnp.minimum(
            jnp.searchsorted(pad_ends, blk * EXPERT_ROWS, side='right'), N_EXPERTS - 1).astype(I32)

        n_used = n_used.reshape(1)
        last_blk = jnp.where(padded > 0, pad_ends // EXPERT_ROWS - 1, -1).astype(I32)
        xs = _dispatch_call(hp, dest_flat, last_blk, n_used, n_blocks, D // 2 // LANES)
        eo = _expert_call(xs, block_e, n_used, w_gate[l], w_up[l], w_down[l], n_blocks)
        xf = _combine_call(dest_flat, wt_t.T, hp, xf, ada[l], ws_gate[l].astype(BF16),
                           ws_up[l].astype(BF16), ws_down[l].astype(BF16), final_norm, eo, S,
                           final=(l == L - 1))
    return xf.reshape(B, S, D)
```

```python
import functools

import jax
import jax.numpy as jnp
from jax import lax
from jax.experimental import pallas as pl
from jax.experimental.pallas import tpu as pltpu

F32 = jnp.float32
BF16 = jnp.bfloat16
U32 = jnp.uint32
I32 = jnp.int32

HEAD_DIM = 128
ROPE_DIM = HEAD_DIM // 4
ROPE_THETA = 500000.0
MOBA_BLOCK = 256
MOBA_TOPK = 3
CONV_LEN = 3
N_EXPERTS = 64
MOE_TOP_K = 8
N_GROUPS = 8
GROUP_SIZE = N_EXPERTS // N_GROUPS
TOPK_GROUPS = 4
ROUTED_SCALE = 2.5
NORM_EPS = 1e-6

VMEM_LIMIT_BYTES = 56 * 1024 * 1024
SUBLANES = 8
EXPERT_ROWS = 256
ATTN_HEAD_GROUP = 4
NEG_BIG = -1e30

NT_DIMS = (((1,), (1,)), ((), ()))


def _cparams(n_axes):
    return pltpu.CompilerParams(
        dimension_semantics=("arbitrary",) * n_axes,
        vmem_limit_bytes=VMEM_LIMIT_BYTES)


def _bits(v):
    return lax.bitcast_convert_type(v, U32)


def _pack_halves(v):
    n = v.shape[1] // 2
    lo = _bits(v[:, :n].astype(BF16).astype(F32))
    hi = _bits(v[:, n:].astype(BF16).astype(F32))
    return hi | (lo >> 16)


def _unpack_lo(w):
    return lax.bitcast_convert_type(w << 16, F32)


def _unpack_hi(w):
    return lax.bitcast_convert_type(w & jnp.uint32(0xFFFF0000), F32)


LANES = 128


def _load_rows(ref, n, ch):
    return jnp.concatenate([ref[pl.ds(s, n, stride=ch), :] for s in range(ch)], axis=1)


def _store_rows(ref, val, n, ch):
    for s in range(ch):
        ref[pl.ds(s, n, stride=ch), :] = val[:, s * LANES:(s + 1) * LANES]


def _silu(v):
    return v * jax.nn.sigmoid(v)


def _rms_modulate(x, g, shift, scale):
    y = x * lax.rsqrt(jnp.mean(x * x, axis=-1, keepdims=True) + NORM_EPS)
    return (y * g) * (1.0 + scale) + shift


def _ada_kernel(c_ref, w_ref, b_ref, o_ref):
    ca = _silu(c_ref[...]).astype(BF16)
    o_ref[0] = jnp.dot(ca, w_ref[0].astype(BF16), preferred_element_type=F32) + b_ref[0]


def _ada_call(c, w_ada, b_ada):
    L, D, N = w_ada.shape
    B = c.shape[0]
    tn = min(512, N)
    return pl.pallas_call(
        _ada_kernel,
        grid=(L, N // tn),
        in_specs=[
            pl.BlockSpec((B, D), lambda l, j: (0, 0)),
            pl.BlockSpec((1, D, tn), lambda l, j: (l, 0, j)),
            pl.BlockSpec((1, 1, tn), lambda l, j: (l, 0, j)),
        ],
        out_specs=pl.BlockSpec((1, B, tn), lambda l, j: (l, 0, j)),
        out_shape=jax.ShapeDtypeStruct((L, B, N), F32),
        compiler_params=_cparams(2),
        name="ada",
    )(c, w_ada, b_ada.reshape(L, 1, N))


def _inproj_kernel(x_ref, g_ref, ada_ref, w_ref, o_ref, h_sc):
    @pl.when(pl.program_id(1) == 0)
    def _():
        h = _rms_modulate(x_ref[...], g_ref[...], ada_ref[0:1, :], ada_ref[1:2, :])
        h_sc[...] = h.astype(BF16)

    o_ref[...] = jnp.dot(h_sc[...], w_ref[...], preferred_element_type=F32).astype(o_ref.dtype)


def _inproj_call(x, g, ada_l, w, S):
    T, D = x.shape
    N = w.shape[1]
    tm = min(1024, S)
    tn = min(512, N)
    return pl.pallas_call(
        _inproj_kernel,
        grid=(T // tm, N // tn),
        in_specs=[
            pl.BlockSpec((tm, D), lambda i, j: (i, 0)),
            pl.BlockSpec((1, D), lambda i, j: (0, 0)),
            pl.BlockSpec((None, 6, D), lambda i, j: ((i * tm) // S, 0, 0)),
            pl.BlockSpec((D, tn), lambda i, j: (0, j)),
        ],
        out_specs=pl.BlockSpec((tm, tn), lambda i, j: (i, j)),
        out_shape=jax.ShapeDtypeStruct((T, N), BF16),
        scratch_shapes=[pltpu.VMEM((tm, D), BF16)],
        compiler_params=_cparams(2),
        name="inproj",
    )(x, g.reshape(1, D), ada_l, w)


def _rope(v, c, s):
    half = ROPE_DIM // 2
    lane = lax.broadcasted_iota(I32, v.shape, 1)
    partner = jnp.where(lane < half,
                        pltpu.roll(v, HEAD_DIM - half, axis=1),
                        pltpu.roll(v, half, axis=1))
    return v * c + partner * s


def _attn_kernel(q_ref, k_ref, v_ref, cq_ref, sq_ref, ck_ref, sk_ref, o_ref,
                 kr_sc, vt_sc, km_sc, sel_sc, qs_sc, s_sc, m_sc, l_sc, acc_sc, *, nb, hg):
    qi = pl.program_id(2)
    blk = MOBA_BLOCK
    dh = HEAD_DIM
    cols = lambda h: slice(h * dh, (h + 1) * dh)

    @pl.when(qi == 0)
    def _():
        km_sc[...] = jnp.zeros_like(km_sc)
        for j in range(nb):
            rows = pl.ds(j * blk, blk)
            ck = ck_ref[rows, :]
            sk = sk_ref[rows, :]
            for h in range(hg):
                kr = _rope(k_ref[rows, cols(h)].astype(F32), ck, sk)
                kr_sc[h, j] = kr.astype(BF16)
                km_sc[h, j:j + 1, :] = jnp.mean(kr, axis=0, keepdims=True)
                vt_sc[h, j] = v_ref[rows, cols(h)].astype(F32).T.astype(BF16)

    cq = cq_ref[...]
    sq = sq_ref[...]
    kidx = lax.broadcasted_iota(I32, (blk, blk), 0)
    qidx = lax.broadcasted_iota(I32, (blk, blk), 1)
    causal = kidx <= qidx
    for h in range(hg):
        q = _rope(q_ref[:, cols(h)].astype(F32), cq, sq)
        sel_sc[h] = lax.dot_general(km_sc[h], q, NT_DIMS, precision=lax.Precision.HIGHEST,
                                    preferred_element_type=F32)
        qs = (q * (HEAD_DIM ** -0.5)).astype(BF16)
        qs_sc[h] = qs
        st = lax.dot_general(kr_sc[h, qi], qs, NT_DIMS, preferred_element_type=F32)
        s_sc[h] = jnp.where(causal, st, NEG_BIG)

    for h in range(hg):
        gt = sel_sc[h]
        row = lax.broadcasted_iota(I32, gt.shape, 0)
        past = row < qi
        gm = jnp.where(past, gt, -jnp.inf)
        rank = jnp.zeros(gt.shape, F32)
        for jp in range(nb):
            gj = gm[jp:jp + 1, :]
            beats = (gj > gm) | ((gj == gm) & (jp < row))
            rank = rank + beats.astype(F32)
        sel_sc[h] = (past & (rank < MOBA_TOPK)).astype(F32)

    for h in range(hg):
        st = s_sc[h]
        m0 = jnp.max(st, axis=0, keepdims=True)
        p0 = jnp.exp(st - m0)
        m_sc[h] = m0
        l_sc[h] = jnp.sum(p0, axis=0, keepdims=True)
        acc_sc[h] = jnp.dot(vt_sc[h, qi], p0.astype(BF16), preferred_element_type=F32)

    def past_block(j, carry):
        for h in range(hg):
            sj = lax.dot_general(kr_sc[h, j], qs_sc[h], NT_DIMS, preferred_element_type=F32)
            s_sc[h] = jnp.where(sel_sc[h, pl.ds(j, 1), :] > 0.0, sj, NEG_BIG)
        for h in range(hg):
            sj = s_sc[h]
            m_old = m_sc[h]
            m_new = jnp.maximum(m_old, jnp.max(sj, axis=0, keepdims=True))
            alpha = jnp.exp(m_old - m_new)
            p = jnp.exp(sj - m_new)
            l_sc[h] = alpha * l_sc[h] + jnp.sum(p, axis=0, keepdims=True)
            acc_sc[h] = alpha * acc_sc[h] + jnp.dot(vt_sc[h, j], p.astype(BF16),
                                                    preferred_element_type=F32)
            m_sc[h] = m_new
        return carry

    lax.fori_loop(0, qi, past_block, 0)

    for h in range(hg):
        o = acc_sc[h] / l_sc[h]
        o_ref[:, cols(h)] = o.T.astype(o_ref.dtype)


def _attn_call(proj, rope_c, rope_s, B, S, H):
    T = B * S
    blk = MOBA_BLOCK
    nb = S // blk
    nbp = -(-nb // SUBLANES) * SUBLANES
    dh = HEAD_DIM
    hg = min(ATTN_HEAD_GROUP, H)
    ng = H // hg
    w = hg * dh
    kern = functools.partial(_attn_kernel, nb=nb, hg=hg)
    return pl.pallas_call(
        kern,
        grid=(B, ng, nb),
        in_specs=[
            pl.BlockSpec((blk, w), lambda b, g, i: (b * nb + i, g)),
            pl.BlockSpec((S, w), lambda b, g, i: (b, ng + g)),
            pl.BlockSpec((S, w), lambda b, g, i: (b, 2 * ng + g)),
            pl.BlockSpec((blk, dh), lambda b, g, i: (b * nb + i, 0)),
            pl.BlockSpec((blk, dh), lambda b, g, i: (b * nb + i, 0)),
            pl.BlockSpec((S, dh), lambda b, g, i: (b, 0)),
            pl.BlockSpec((S, dh), lambda b, g, i: (b, 0)),
        ],
        out_specs=pl.BlockSpec((blk, w), lambda b, g, i: (b * nb + i, g)),
        out_shape=jax.ShapeDtypeStruct((T, H * dh), BF16),
        scratch_shapes=[
            pltpu.VMEM((hg, nb, blk, dh), BF16),
            pltpu.VMEM((hg, nb, dh, blk), BF16),
            pltpu.VMEM((hg, nbp, dh), F32),
            pltpu.VMEM((hg, nbp, blk), F32),
            pltpu.VMEM((hg, blk, dh), BF16),
            pltpu.VMEM((hg, blk, blk), F32),
            pltpu.VMEM((hg, 1, blk), F32),
            pltpu.VMEM((hg, 1, blk), F32),
            pltpu.VMEM((hg, dh, blk), F32),
        ],
        compiler_params=_cparams(3),
        name="moba_attn",
    )(proj, proj, proj, rope_c, rope_s, rope_c, rope_s)


def _mix1_kernel(ya_ref, cb_ref, cc_ref, cx_ref, hcc_ref, hcx_ref, ga_ref, gc_ref, cw_ref,
                 wb0_ref, wb1_ref, o_ref, *, tm, S):
    i = pl.program_id(0)
    u = cc_ref[...].astype(F32) * cx_ref[...].astype(F32)
    hu = hcc_ref[...].astype(F32) * hcx_ref[...].astype(F32)
    seq_start = (i * tm) % S == 0
    hu = jnp.where(seq_start, 0.0, hu)
    h1 = hu[SUBLANES - 1:SUBLANES, :]
    h2 = hu[SUBLANES - 2:SUBLANES - 1, :]
    r = lax.broadcasted_iota(I32, u.shape, 0)
    u1 = jnp.where(r == 0, h1, pltpu.roll(u, 1, axis=0))
    u2 = jnp.where(r == 0, h2, jnp.where(r == 1, h1, pltpu.roll(u, 2, axis=0)))
    conv = cw_ref[0:1, :] * u2 + cw_ref[1:2, :] * u1 + cw_ref[2:3, :] * u
    yc = (cb_ref[...].astype(F32) * conv).astype(BF16)
    a = jnp.dot(ya_ref[...], wb0_ref[...], preferred_element_type=F32)
    c = jnp.dot(yc, wb1_ref[...], preferred_element_type=F32)
    merged = (jax.nn.sigmoid(ga_ref[...].astype(F32)) * a
              + jax.nn.sigmoid(gc_ref[...].astype(F32)) * c)
    o_ref[...] = merged.astype(o_ref.dtype)


def _mix1_call(ya, proj, conv_w, wb0, wb1, S, D):
    T, A = ya.shape
    assert conv_w.shape == (CONV_LEN, A) and (6 * A) % D == 0
    tm = min(256, S)
    gcol = 6 * A // D
    hb = tm // SUBLANES
    kern = functools.partial(_mix1_kernel, tm=tm, S=S)
    halo = lambda col: pl.BlockSpec((SUBLANES, A), lambda i: (jnp.maximum(i * hb - 1, 0), col))
    return pl.pallas_call(
        kern,
        grid=(T // tm,),
        in_specs=[
            pl.BlockSpec((tm, A), lambda i: (i, 0)),
            pl.BlockSpec((tm, A), lambda i: (i, 3)),
            pl.BlockSpec((tm, A), lambda i: (i, 4)),
            pl.BlockSpec((tm, A), lambda i: (i, 5)),
            halo(4),
            halo(5),
            pl.BlockSpec((tm, D), lambda i: (i, gcol)),
            pl.BlockSpec((tm, D), lambda i: (i, gcol + 1)),
            pl.BlockSpec((CONV_LEN, A), lambda i: (0, 0)),
            pl.BlockSpec((A, D), lambda i: (0, 0)),
            pl.BlockSpec((A, D), lambda i: (0, 0)),
        ],
        out_specs=pl.BlockSpec((tm, D), lambda i: (i, 0)),
        out_shape=jax.ShapeDtypeStruct((T, D), BF16),
        compiler_params=_cparams(1),
        name="mix_merge",
    )(ya, proj, proj, proj, proj, proj, proj, proj, conv_w, wb0, wb1)


def _mix2_kernel(m_ref, wo_ref, x_ref, ada_ref, g_ref, wrt_ref, xo_ref, hp_ref, lg_ref):
    out = jnp.dot(m_ref[...], wo_ref[...], preferred_element_type=F32)
    x = x_ref[...] + ada_ref[2:3, :] * out
    xo_ref[...] = x
    h = _rms_modulate(x, g_ref[...], ada_ref[3:4, :], ada_ref[4:5, :])
    tm, d = x.shape
    _store_rows(hp_ref, _pack_halves(h), tm, d // 2 // LANES)
    lg_ref[...] = lax.dot_general(wrt_ref[...], h, NT_DIMS, precision=lax.Precision.HIGHEST,
                                  preferred_element_type=F32)


def _mix2_call(merged, w_out, x, ada_l, g, wrt, S):
    T, D = x.shape
    tm = min(256, S)
    E = wrt.shape[0]
    ch = D // 2 // LANES
    return pl.pallas_call(
        _mix2_kernel,
        grid=(T // tm,),
        in_specs=[
            pl.BlockSpec((tm, D), lambda i: (i, 0)),
            pl.BlockSpec((D, D), lambda i: (0, 0)),
            pl.BlockSpec((tm, D), lambda i: (i, 0)),
            pl.BlockSpec((None, 6, D), lambda i: ((i * tm) // S, 0, 0)),
            pl.BlockSpec((1, D), lambda i: (0, 0)),
            pl.BlockSpec((E, D), lambda i: (0, 0)),
        ],
        out_specs=[
            pl.BlockSpec((tm, D), lambda i: (i, 0)),
            pl.BlockSpec((tm * ch, LANES), lambda i: (i, 0)),
            pl.BlockSpec((E, tm), lambda i: (0, i)),
        ],
        out_shape=[
            jax.ShapeDtypeStruct((T, D), F32),
            jax.ShapeDtypeStruct((T * ch, LANES), U32),
            jax.ShapeDtypeStruct((E, T), F32),
        ],
        compiler_params=_cparams(1),
        name="outproj_norm_router",
    )(merged, w_out, x, ada_l, g.reshape(1, D), wrt)


def _route_kernel(lg_ref, bias_ref, tri_ref, idx_ref, wt_ref, rs_ref, cnt_ref, carry_sc):
    step = pl.program_id(0)

    @pl.when(step == 0)
    def _():
        carry_sc[...] = jnp.zeros_like(carry_sc)

    G = N_GROUPS
    scores = jax.nn.sigmoid(lg_ref[...])
    biased = scores + bias_ref[...]
    sc = [scores[i * G:(i + 1) * G, :] for i in range(GROUP_SIZE)]
    bi = [biased[i * G:(i + 1) * G, :] for i in range(GROUP_SIZE)]
    grow = lax.broadcasted_iota(I32, bi[0].shape, 0)
    eid = [grow * GROUP_SIZE + i for i in range(GROUP_SIZE)]
    neg_inf = jnp.float32(-jnp.inf)

    m1 = functools.reduce(jnp.maximum, bi)
    first = jnp.full(grow.shape, GROUP_SIZE, I32)
    for i in reversed(range(GROUP_SIZE)):
        first = jnp.where(bi[i] == m1, i, first)
    m2 = functools.reduce(jnp.maximum,
                          [jnp.where(first == i, neg_inf, bi[i]) for i in range(GROUP_SIZE)])
    gs = m1 + m2

    grank = jnp.zeros(gs.shape, F32)
    for gp in range(G):
        v = gs[gp:gp + 1, :]
        grank = grank + ((v > gs) | ((v == gs) & (gp < grow))).astype(F32)
    gsel = grank < TOPK_GROUPS

    ms = [jnp.where(gsel, b, neg_inf) for b in bi]
    chosen = [jnp.zeros(grow.shape, jnp.bool_) for _ in range(GROUP_SIZE)]
    picked_idx = []
    picked_score = []
    for _ in range(MOE_TOP_K):
        mx = jnp.max(functools.reduce(jnp.maximum, ms), axis=0, keepdims=True)
        cand = functools.reduce(
            jnp.minimum, [jnp.where(ms[i] == mx, eid[i], N_EXPERTS) for i in range(GROUP_SIZE)])
        ce = jnp.min(cand, axis=0, keepdims=True)
        hit = [eid[i] == ce for i in range(GROUP_SIZE)]
        s_k = jnp.sum(functools.reduce(
            jnp.add, [jnp.where(hit[i], sc[i], 0.0) for i in range(GROUP_SIZE)]),
            axis=0, keepdims=True)
        ms = [jnp.where(hit[i], neg_inf, ms[i]) for i in range(GROUP_SIZE)]
        chosen = [chosen[i] | hit[i] for i in range(GROUP_SIZE)]
        picked_idx.append(ce)
        picked_score.append(s_k)

    wsum = functools.reduce(jnp.add, picked_score)
    for k in range(MOE_TOP_K):
        idx_ref[k:k + 1, :] = picked_idx[k]
        wt_ref[k:k + 1, :] = picked_score[k] / wsum * ROUTED_SCALE

    selm = jnp.concatenate([c.astype(F32) for c in chosen], axis=0)
    pre = jnp.dot(selm.astype(BF16), tri_ref[...], preferred_element_type=F32) + carry_sc[:, 0:1]
    pre_slab = [pre[i * G:(i + 1) * G, :] for i in range(GROUP_SIZE)]
    for k in range(MOE_TOP_K):
        rk = jnp.sum(functools.reduce(
            jnp.add, [jnp.where(eid[i] == picked_idx[k], pre_slab[i], 0.0)
                      for i in range(GROUP_SIZE)]), axis=0, keepdims=True)
        rs_ref[k:k + 1, :] = rk.astype(I32)
    carry_sc[...] = carry_sc[...] + jnp.sum(selm, axis=1, keepdims=True)
    cnt_ref[...] = carry_sc[...]


def _route_call(lgt, bias_perm, T):
    E = lgt.shape[0]
    tr = min(512, T)
    tri = (lax.broadcasted_iota(I32, (tr, tr), 0) < lax.broadcasted_iota(I32, (tr, tr), 1)).astype(BF16)
    return pl.pallas_call(
        _route_kernel,
        grid=(T // tr,),
        in_specs=[
            pl.BlockSpec((E, tr), lambda i: (0, i)),
            pl.BlockSpec((E, 1), lambda i: (0, 0)),
            pl.BlockSpec((tr, tr), lambda i: (0, 0)),
        ],
        out_specs=[
            pl.BlockSpec((MOE_TOP_K, tr), lambda i: (0, i)),
            pl.BlockSpec((MOE_TOP_K, tr), lambda i: (0, i)),
            pl.BlockSpec((MOE_TOP_K, tr), lambda i: (0, i)),
            pl.BlockSpec((E, 128), lambda i: (0, 0)),
        ],
        out_shape=[
            jax.ShapeDtypeStruct((MOE_TOP_K, T), I32),
            jax.ShapeDtypeStruct((MOE_TOP_K, T), F32),
            jax.ShapeDtypeStruct((MOE_TOP_K, T), I32),
            jax.ShapeDtypeStruct((E, 128), F32),
        ],
        scratch_shapes=[pltpu.VMEM((E, 128), F32)],
        compiler_params=_cparams(1),
        name="route",
    )(lgt, bias_perm.reshape(E, 1), tri)


def _dispatch_kernel(zs_ref, nu_ref, hp_ref, dest_ref, xs_ref, z_sc, sem, *, td, ch, n_blocks):
    i = pl.program_id(0)
    zrows = EXPERT_ROWS * ch

    @pl.when(i == 0)
    def _():
        z_sc[...] = jnp.zeros_like(z_sc)
        zero_wait = pltpu.make_async_copy(z_sc, xs_ref.at[pl.ds(0, zrows), :], sem)

        def zero_block(b, carry):
            b0 = pl.multiple_of(b * zrows, zrows)
            pltpu.make_async_copy(z_sc, xs_ref.at[pl.ds(b0, zrows), :], sem).start()
            return carry

        def wait_block(b, carry):
            zero_wait.wait()
            return carry

        for e in range(N_EXPERTS):
            @pl.when(zs_ref[e] >= 0)
            def _(e=e):
                zero_block(zs_ref[e], 0)
        lax.fori_loop(nu_ref[0], n_blocks, zero_block, 0)
        for e in range(N_EXPERTS):
            @pl.when(zs_ref[e] >= 0)
            def _():
                zero_wait.wait()
        lax.fori_loop(nu_ref[0], n_blocks, wait_block, 0)

    def issue(t, carry):
        src = hp_ref.at[pl.ds(pl.multiple_of(t * ch, ch), ch), :]
        for k in range(MOE_TOP_K):
            r0 = pl.multiple_of(dest_ref[t * MOE_TOP_K + k] * ch, ch)
            pltpu.make_async_copy(src, xs_ref.at[pl.ds(r0, ch), :], sem).start()
        return carry

    lax.fori_loop(0, td, issue, 0)
    for k in range(MOE_TOP_K):
        pltpu.make_async_copy(hp_ref, xs_ref.at[pl.ds(0, td * ch), :], sem).wait()


def _dispatch_call(hp, dest_flat, zstart, n_used, n_blocks, ch):
    T = hp.shape[0] // ch
    td = min(256, T)
    n_rows = n_blocks * EXPERT_ROWS
    kern = functools.partial(_dispatch_kernel, td=td, ch=ch, n_blocks=n_blocks)
    return pl.pallas_call(
        kern,
        grid_spec=pltpu.PrefetchScalarGridSpec(
            num_scalar_prefetch=2,
            grid=(T // td,),
            in_specs=[
                pl.BlockSpec((td * ch, LANES), lambda i, zs, nu: (i, 0)),
                pl.BlockSpec((td * MOE_TOP_K,), lambda i, zs, nu: (i,), memory_space=pltpu.SMEM),
            ],
            out_specs=pl.BlockSpec(memory_space=pl.ANY),
            scratch_shapes=[pltpu.VMEM((EXPERT_ROWS * ch, LANES), U32), pltpu.SemaphoreType.DMA],
        ),
        out_shape=jax.ShapeDtypeStruct((n_rows * ch, LANES), U32),
        compiler_params=_cparams(1),
        name="dispatch",
    )(zstart, n_used, hp, dest_flat)


def _expert_kernel(be_ref, nu_ref, xs_ref, wg_ref, wu_ref, wd_ref, o_ref, wg_sc, wu_sc, wd_sc):
    i = pl.program_id(0)
    prev = be_ref[jnp.maximum(i - 1, 0)]
    changed = (i == 0) | (be_ref[i] != prev)

    @pl.when(changed)
    def _():
        wg_sc[...] = wg_ref[...].astype(BF16)
        wu_sc[...] = wu_ref[...].astype(BF16)
        wd_sc[...] = wd_ref[...].astype(BF16)

    @pl.when(i < nu_ref[0])
    def _():
        half = wg_sc.shape[0] // 2
        ch = half // LANES
        w = _load_rows(xs_ref, EXPERT_ROWS, ch)
        xlo = _unpack_lo(w).astype(BF16)
        xhi = _unpack_hi(w).astype(BF16)
        gate = (jnp.dot(xlo, wg_sc[:half, :], preferred_element_type=F32)
                + jnp.dot(xhi, wg_sc[half:, :], preferred_element_type=F32))
        up = (jnp.dot(xlo, wu_sc[:half, :], preferred_element_type=F32)
              + jnp.dot(xhi, wu_sc[half:, :], preferred_element_type=F32))
        act = (_silu(gate) * up).astype(BF16)
        out = jnp.dot(act, wd_sc[...], preferred_element_type=F32)
        _store_rows(o_ref, _pack_halves(out), EXPERT_ROWS, ch)

    @pl.when(i >= nu_ref[0])
    def _():
        o_ref[...] = jnp.zeros_like(o_ref)


def _expert_call(xs, block_e, n_used, w_gate, w_up, w_down, layer, n_blocks):
    _, E, D, F = w_gate.shape
    ch = D // 2 // LANES
    rows = lambda i, be, nu: (jnp.minimum(i, nu[0] - 1), 0)
    out_rows = lambda i, be, nu: (i, 0)
    wmap = lambda i, be, nu: (layer, be[i], 0, 0)
    return pl.pallas_call(
        _expert_kernel,
        grid_spec=pltpu.PrefetchScalarGridSpec(
            num_scalar_prefetch=2,
            grid=(n_blocks,),
            in_specs=[
                pl.BlockSpec((EXPERT_ROWS * ch, LANES), rows),
                pl.BlockSpec((None, None, D, F), wmap),
                pl.BlockSpec((None, None, D, F), wmap),
                pl.BlockSpec((None, None, F, D), wmap),
            ],
            out_specs=pl.BlockSpec((EXPERT_ROWS * ch, LANES), out_rows),
            scratch_shapes=[pltpu.VMEM((D, F), BF16), pltpu.VMEM((D, F), BF16),
                            pltpu.VMEM((F, D), BF16)],
        ),
        out_shape=jax.ShapeDtypeStruct((n_blocks * EXPERT_ROWS * ch, LANES), U32),
        compiler_params=_cparams(1),
        name="experts",
    )(block_e, n_used, xs, w_gate, w_up, w_down)


def _combine_kernel(dest_ref, wt_ref, hp_ref, x_ref, ada_ref, wsg_ref, wsu_ref, wsd_ref, fn_ref,
                    eo_ref, xo_ref, gbuf, sem, *, tc, final):
    half = x_ref.shape[1] // 2
    ch = half // LANES

    def issue(t, carry):
        t0 = pl.multiple_of(t * ch, ch)
        for k in range(MOE_TOP_K):
            r0 = pl.multiple_of(dest_ref[t * MOE_TOP_K + k] * ch, ch)
            pltpu.make_async_copy(eo_ref.at[pl.ds(r0, ch), :], gbuf.at[k, pl.ds(t0, ch), :],
                                  sem).start()
        return carry

    lax.fori_loop(0, tc, issue, 0)

    w = _load_rows(hp_ref, tc, ch)
    hlo = _unpack_lo(w).astype(BF16)
    hhi = _unpack_hi(w).astype(BF16)
    gate = (jnp.dot(hlo, wsg_ref[:half, :], preferred_element_type=F32)
            + jnp.dot(hhi, wsg_ref[half:, :], preferred_element_type=F32))
    up = (jnp.dot(hlo, wsu_ref[:half, :], preferred_element_type=F32)
          + jnp.dot(hhi, wsu_ref[half:, :], preferred_element_type=F32))
    act = (_silu(gate) * up).astype(BF16)
    shared = jnp.dot(act, wsd_ref[...], preferred_element_type=F32)

    for k in range(MOE_TOP_K):
        pltpu.make_async_copy(eo_ref.at[pl.ds(0, tc * ch), :], gbuf.at[k], sem).wait()

    wts = wt_ref[...]
    ylo = shared[:, :half]
    yhi = shared[:, half:]
    for k in range(MOE_TOP_K):
        g = _load_rows(gbuf.at[k], tc, ch)
        wk = wts[:, k:k + 1]
        ylo = ylo + wk * _unpack_lo(g)
        yhi = yhi + wk * _unpack_hi(g)
    gf = ada_ref[5:6, :]
    xlo = x_ref[:, :half] + gf[:, :half] * ylo
    xhi = x_ref[:, half:] + gf[:, half:] * yhi
    if final:
        ss = (jnp.sum(xlo * xlo, axis=-1, keepdims=True)
              + jnp.sum(xhi * xhi, axis=-1, keepdims=True))
        inv = lax.rsqrt(ss / (2 * half) + NORM_EPS)
        xlo = xlo * inv * fn_ref[:, :half]
        xhi = xhi * inv * fn_ref[:, half:]
    xo_ref[:, :half] = xlo
    xo_ref[:, half:] = xhi


def _combine_call(dest_flat, wts, hp, x, ada_l, wsg, wsu, wsd, fnorm, eo, S, final):
    T, D = x.shape
    ch = D // 2 // LANES
    Fs = wsg.shape[1]
    tc = min(128, S)
    kern = functools.partial(_combine_kernel, tc=tc, final=final)
    return pl.pallas_call(
        kern,
        grid=(T // tc,),
        in_specs=[
            pl.BlockSpec((tc * MOE_TOP_K,), lambda i: (i,), memory_space=pltpu.SMEM),
            pl.BlockSpec((tc, MOE_TOP_K), lambda i: (i, 0)),
            pl.BlockSpec((tc * ch, LANES), lambda i: (i, 0)),
            pl.BlockSpec((tc, D), lambda i: (i, 0)),
            pl.BlockSpec((None, 6, D), lambda i: ((i * tc) // S, 0, 0)),
            pl.BlockSpec((D, Fs), lambda i: (0, 0)),
            pl.BlockSpec((D, Fs), lambda i: (0, 0)),
            pl.BlockSpec((Fs, D), lambda i: (0, 0)),
            pl.BlockSpec((1, D), lambda i: (0, 0)),
            pl.BlockSpec(memory_space=pl.ANY),
        ],
        out_specs=pl.BlockSpec((tc, D), lambda i: (i, 0)),
        out_shape=jax.ShapeDtypeStruct((T, D), F32),
        scratch_shapes=[pltpu.VMEM((MOE_TOP_K, tc * ch, LANES), U32), pltpu.SemaphoreType.DMA],
        compiler_params=_cparams(1),
        name="combine_shared",
    )(dest_flat, wts, hp, x, ada_l, wsg, wsu, wsd, fnorm.reshape(1, D), eo)


def _rope_tables(positions):
    half = ROPE_DIM // 2
    inv_freq = ROPE_THETA ** (-jnp.arange(0, ROPE_DIM, 2, dtype=F32) / ROPE_DIM)
    ang = positions.astype(F32).reshape(-1, 1) * inv_freq
    cos, sin = jnp.cos(ang), jnp.sin(ang)
    T = ang.shape[0]
    rest = HEAD_DIM - ROPE_DIM
    c = jnp.concatenate([cos, cos, jnp.ones((T, rest), F32)], axis=-1)
    s = jnp.concatenate([-sin, sin, jnp.zeros((T, rest), F32)], axis=-1)
    return c, s


def _expert_row_perm():
    r = jnp.arange(N_EXPERTS)
    return (r % N_GROUPS) * GROUP_SIZE + r // N_GROUPS


def kernel(x, c, positions, norm_mix, w_ada, b_ada, w_in, conv_w, w_branch, w_out, norm_ffn, w_router, router_bias, w_gate, w_up, w_down, ws_gate, ws_up, ws_down, final_norm):
    B, S, D = x.shape
    L = w_ada.shape[0]
    T = B * S
    A = w_branch.shape[2]
    H = A // HEAD_DIM
    assert S % MOBA_BLOCK == 0 and w_in.shape[2] == 6 * A + 2 * D

    n_assign = T * MOE_TOP_K
    n_blocks = -(-(n_assign + N_EXPERTS * (EXPERT_ROWS - 1)) // EXPERT_ROWS)

    rope_c, rope_s = _rope_tables(positions)
    ada = _ada_call(c, w_ada, b_ada).reshape(L, B, 6, D)
    perm = _expert_row_perm()
    xf = x.reshape(T, D)

    for l in range(L):
        proj = _inproj_call(xf, norm_mix[l], ada[l], w_in[l].astype(BF16), S)
        ya = _attn_call(proj, rope_c, rope_s, B, S, H)
        merged = _mix1_call(ya, proj, conv_w[l], w_branch[l, 0].astype(BF16),
                            w_branch[l, 1].astype(BF16), S, D)
        xf, hp, lgt = _mix2_call(merged, w_out[l].astype(BF16), xf, ada[l], norm_ffn[l],
                                 w_router[l].T[perm], S)

        idx_t, wt_t, rs_t, cnt = _route_call(lgt, router_bias[l][perm], T)
        counts = cnt[perm, 0].astype(I32)
        padded = (counts + EXPERT_ROWS - 1) // EXPERT_ROWS * EXPERT_ROWS
        pad_ends = jnp.cumsum(padded)
        pad_starts = pad_ends - padded
        seg_start = jnp.sum(jnp.where(idx_t[:, :, None] == jnp.arange(N_EXPERTS), pad_starts, 0),
                            axis=-1)
        dest_flat = (seg_start + rs_t).T.reshape(-1)
        n_used = (pad_ends[-1] // EXPERT_ROWS).astype(I32)
        blk = jnp.minimum(jnp.arange(n_blocks, dtype=I32), n_used - 1)
        block_e = jnp.minimum(
            jnp.sum(pad_ends[None, :] <= (blk * EXPERT_ROWS)[:, None], axis=1),
            N_EXPERTS - 1).astype(I32)

        n_used = n_used.reshape(1)
        last_blk = jnp.where(padded > 0, pad_ends // EXPERT_ROWS - 1, -1).astype(I32)
        xs = _dispatch_call(hp, dest_flat, last_blk, n_used, n_blocks, D // 2 // LANES)
        eo = _expert_call(xs, block_e, n_used, w_gate, w_up, w_down, l, n_blocks)
        xf = _combine_call(dest_flat, wt_t.T, hp, xf, ada[l], ws_gate[l].astype(BF16),
                           ws_up[l].astype(BF16), ws_down[l].astype(BF16), final_norm, eo, S,
                           final=(l == L - 1))
    return xf.reshape(B, S, D)
```

```python
import functools

import jax
import jax.numpy as jnp
from jax import lax
from jax.experimental import pallas as pl
from jax.experimental.pallas import tpu as pltpu

F32 = jnp.float32
BF16 = jnp.bfloat16
U32 = jnp.uint32
I32 = jnp.int32

HEAD_DIM = 128
ROPE_DIM = HEAD_DIM // 4
ROPE_THETA = 500000.0
MOBA_BLOCK = 256
MOBA_TOPK = 3
CONV_LEN = 3
N_EXPERTS = 64
MOE_TOP_K = 8
N_GROUPS = 8
GROUP_SIZE = N_EXPERTS // N_GROUPS
TOPK_GROUPS = 4
ROUTED_SCALE = 2.5
NORM_EPS = 1e-6

VMEM_LIMIT_BYTES = 56 * 1024 * 1024
SUBLANES = 8
EXPERT_ROWS = 256
ATTN_HEAD_GROUP = 4
NEG_BIG = -1e30

NT_DIMS = (((1,), (1,)), ((), ()))


def _cparams(n_axes):
    return pltpu.CompilerParams(
        dimension_semantics=("arbitrary",) * n_axes,
        vmem_limit_bytes=VMEM_LIMIT_BYTES)


def _bits(v):
    return lax.bitcast_convert_type(v, U32)


def _pack_halves(v):
    n = v.shape[1] // 2
    lo = _bits(v[:, :n].astype(BF16).astype(F32))
    hi = _bits(v[:, n:].astype(BF16).astype(F32))
    return hi | (lo >> 16)


def _unpack_lo(w):
    return lax.bitcast_convert_type(w << 16, F32)


def _unpack_hi(w):
    return lax.bitcast_convert_type(w & jnp.uint32(0xFFFF0000), F32)


LANES = 128


def _load_rows(ref, n, ch):
    return jnp.concatenate([ref[pl.ds(s, n, stride=ch), :] for s in range(ch)], axis=1)


def _store_rows(ref, val, n, ch):
    for s in range(ch):
        ref[pl.ds(s, n, stride=ch), :] = val[:, s * LANES:(s + 1) * LANES]


def _silu(v):
    return v * jax.nn.sigmoid(v)


def _rms_modulate(x, g, shift, scale):
    y = x * lax.rsqrt(jnp.mean(x * x, axis=-1, keepdims=True) + NORM_EPS)
    return (y * g) * (1.0 + scale) + shift


def _ada_kernel(c_ref, w_ref, b_ref, o_ref):
    ca = _silu(c_ref[...]).astype(BF16)
    o_ref[0] = jnp.dot(ca, w_ref[0].astype(BF16), preferred_element_type=F32) + b_ref[0]


def _ada_call(c, w_ada, b_ada):
    L, D, N = w_ada.shape
    B = c.shape[0]
    tn = min(512, N)
    return pl.pallas_call(
        _ada_kernel,
        grid=(L, N // tn),
        in_specs=[
            pl.BlockSpec((B, D), lambda l, j: (0, 0)),
            pl.BlockSpec((1, D, tn), lambda l, j: (l, 0, j)),
            pl.BlockSpec((1, 1, tn), lambda l, j: (l, 0, j)),
        ],
        out_specs=pl.BlockSpec((1, B, tn), lambda l, j: (l, 0, j)),
        out_shape=jax.ShapeDtypeStruct((L, B, N), F32),
        compiler_params=_cparams(2),
        name="ada",
    )(c, w_ada, b_ada.reshape(L, 1, N))


def _inproj_kernel(x_ref, g_ref, ada_ref, w_ref, o_ref, h_sc):
    @pl.when(pl.program_id(1) == 0)
    def _():
        h = _rms_modulate(x_ref[...], g_ref[...], ada_ref[0:1, :], ada_ref[1:2, :])
        h_sc[...] = h.astype(BF16)

    o_ref[...] = jnp.dot(h_sc[...], w_ref[...], preferred_element_type=F32).astype(o_ref.dtype)


def _inproj_call(x, g, ada_l, w, S):
    T, D = x.shape
    N = w.shape[1]
    tm = min(1024, S)
    tn = min(512, N)
    return pl.pallas_call(
        _inproj_kernel,
        grid=(T // tm, N // tn),
        in_specs=[
            pl.BlockSpec((tm, D), lambda i, j: (i, 0)),
            pl.BlockSpec((1, D), lambda i, j: (0, 0)),
            pl.BlockSpec((None, 6, D), lambda i, j: ((i * tm) // S, 0, 0)),
            pl.BlockSpec((D, tn), lambda i, j: (0, j)),
        ],
        out_specs=pl.BlockSpec((tm, tn), lambda i, j: (i, j)),
        out_shape=jax.ShapeDtypeStruct((T, N), BF16),
        scratch_shapes=[pltpu.VMEM((tm, D), BF16)],
        compiler_params=_cparams(2),
        name="inproj",
    )(x, g.reshape(1, D), ada_l, w)


def _rope(v, c, s):
    half = ROPE_DIM // 2
    lane = lax.broadcasted_iota(I32, v.shape, 1)
    partner = jnp.where(lane < half,
                        pltpu.roll(v, HEAD_DIM - half, axis=1),
                        pltpu.roll(v, half, axis=1))
    return v * c + partner * s


def _attn_kernel(q_ref, k_ref, v_ref, cq_ref, sq_ref, ck_ref, sk_ref, o_ref,
                 kr_sc, vt_sc, km_sc, sel_sc, qs_sc, s_sc, m_sc, l_sc, acc_sc, *, nb, hg):
    qi = pl.program_id(2)
    blk = MOBA_BLOCK
    dh = HEAD_DIM
    cols = lambda h: slice(h * dh, (h + 1) * dh)

    @pl.when(qi == 0)
    def _():
        km_sc[...] = jnp.zeros_like(km_sc)
        for j in range(nb):
            rows = pl.ds(j * blk, blk)
            ck = ck_ref[rows, :]
            sk = sk_ref[rows, :]
            for h in range(hg):
                kr = _rope(k_ref[rows, cols(h)].astype(F32), ck, sk)
                kr_sc[h, j] = kr.astype(BF16)
                km_sc[h, j:j + 1, :] = jnp.mean(kr, axis=0, keepdims=True)
                vt_sc[h, j] = v_ref[rows, cols(h)].astype(F32).T.astype(BF16)

    cq = cq_ref[...]
    sq = sq_ref[...]
    kidx = lax.broadcasted_iota(I32, (blk, blk), 0)
    qidx = lax.broadcasted_iota(I32, (blk, blk), 1)
    causal = kidx <= qidx
    for h in range(hg):
        q = _rope(q_ref[:, cols(h)].astype(F32), cq, sq)
        sel_sc[h] = lax.dot_general(km_sc[h], q, NT_DIMS, precision=lax.Precision.HIGHEST,
                                    preferred_element_type=F32)
        qs = (q * (HEAD_DIM ** -0.5)).astype(BF16)
        qs_sc[h] = qs
        st = lax.dot_general(kr_sc[h, qi], qs, NT_DIMS, preferred_element_type=F32)
        s_sc[h] = jnp.where(causal, st, NEG_BIG)

    for h in range(hg):
        gt = sel_sc[h]
        row = lax.broadcasted_iota(I32, gt.shape, 0)
        past = row < qi
        gm = jnp.where(past, gt, -jnp.inf)
        rank = jnp.zeros(gt.shape, F32)
        for jp in range(nb):
            gj = gm[jp:jp + 1, :]
            beats = (gj > gm) | ((gj == gm) & (jp < row))
            rank = rank + beats.astype(F32)
        sel_sc[h] = (past & (rank < MOBA_TOPK)).astype(F32)

    for h in range(hg):
        st = s_sc[h]
        m0 = jnp.max(st, axis=0, keepdims=True)
        p0 = jnp.exp(st - m0)
        m_sc[h] = m0
        l_sc[h] = jnp.sum(p0, axis=0, keepdims=True)
        acc_sc[h] = jnp.dot(vt_sc[h, qi], p0.astype(BF16), preferred_element_type=F32)

    def past_block(j, carry):
        for h in range(hg):
            sj = lax.dot_general(kr_sc[h, j], qs_sc[h], NT_DIMS, preferred_element_type=F32)
            s_sc[h] = jnp.where(sel_sc[h, pl.ds(j, 1), :] > 0.0, sj, NEG_BIG)
        for h in range(hg):
            sj = s_sc[h]
            m_old = m_sc[h]
            m_new = jnp.maximum(m_old, jnp.max(sj, axis=0, keepdims=True))
            alpha = jnp.exp(m_old - m_new)
            p = jnp.exp(sj - m_new)
            l_sc[h] = alpha * l_sc[h] + jnp.sum(p, axis=0, keepdims=True)
            acc_sc[h] = alpha * acc_sc[h] + jnp.dot(vt_sc[h, j], p.astype(BF16),
                                                    preferred_element_type=F32)
            m_sc[h] = m_new
        return carry

    lax.fori_loop(0, qi, past_block, 0)

    for h in range(hg):
        o = acc_sc[h] / l_sc[h]
        o_ref[:, cols(h)] = o.T.astype(o_ref.dtype)


def _attn_call(proj, rope_c, rope_s, B, S, H):
    T = B * S
    blk = MOBA_BLOCK
    nb = S // blk
    nbp = -(-nb // SUBLANES) * SUBLANES
    dh = HEAD_DIM
    hg = min(ATTN_HEAD_GROUP, H)
    ng = H // hg
    w = hg * dh
    kern = functools.partial(_attn_kernel, nb=nb, hg=hg)
    return pl.pallas_call(
        kern,
        grid=(B, ng, nb),
        in_specs=[
            pl.BlockSpec((blk, w), lambda b, g, i: (b * nb + i, g)),
            pl.BlockSpec((S, w), lambda b, g, i: (b, ng + g)),
            pl.BlockSpec((S, w), lambda b, g, i: (b, 2 * ng + g)),
            pl.BlockSpec((blk, dh), lambda b, g, i: (b * nb + i, 0)),
            pl.BlockSpec((blk, dh), lambda b, g, i: (b * nb + i, 0)),
            pl.BlockSpec((S, dh), lambda b, g, i: (b, 0)),
            pl.BlockSpec((S, dh), lambda b, g, i: (b, 0)),
        ],
        out_specs=pl.BlockSpec((blk, w), lambda b, g, i: (b * nb + i, g)),
        out_shape=jax.ShapeDtypeStruct((T, H * dh), BF16),
        scratch_shapes=[
            pltpu.VMEM((hg, nb, blk, dh), BF16),
            pltpu.VMEM((hg, nb, dh, blk), BF16),
            pltpu.VMEM((hg, nbp, dh), F32),
            pltpu.VMEM((hg, nbp, blk), F32),
            pltpu.VMEM((hg, blk, dh), BF16),
            pltpu.VMEM((hg, blk, blk), F32),
            pltpu.VMEM((hg, 1, blk), F32),
            pltpu.VMEM((hg, 1, blk), F32),
            pltpu.VMEM((hg, dh, blk), F32),
        ],
        compiler_params=_cparams(3),
        name="moba_attn",
    )(proj, proj, proj, rope_c, rope_s, rope_c, rope_s)


def _mix1_kernel(ya_ref, cb_ref, cc_ref, cx_ref, hcc_ref, hcx_ref, ga_ref, gc_ref, cw_ref,
                 wb0_ref, wb1_ref, o_ref, *, tm, S):
    i = pl.program_id(0)
    u = cc_ref[...].astype(F32) * cx_ref[...].astype(F32)
    hu = hcc_ref[...].astype(F32) * hcx_ref[...].astype(F32)
    seq_start = (i * tm) % S == 0
    hu = jnp.where(seq_start, 0.0, hu)
    h1 = hu[SUBLANES - 1:SUBLANES, :]
    h2 = hu[SUBLANES - 2:SUBLANES - 1, :]
    r = lax.broadcasted_iota(I32, u.shape, 0)
    u1 = jnp.where(r == 0, h1, pltpu.roll(u, 1, axis=0))
    u2 = jnp.where(r == 0, h2, jnp.where(r == 1, h1, pltpu.roll(u, 2, axis=0)))
    conv = cw_ref[0:1, :] * u2 + cw_ref[1:2, :] * u1 + cw_ref[2:3, :] * u
    yc = (cb_ref[...].astype(F32) * conv).astype(BF16)
    a = jnp.dot(ya_ref[...], wb0_ref[...], preferred_element_type=F32)
    c = jnp.dot(yc, wb1_ref[...], preferred_element_type=F32)
    merged = (jax.nn.sigmoid(ga_ref[...].astype(F32)) * a
              + jax.nn.sigmoid(gc_ref[...].astype(F32)) * c)
    o_ref[...] = merged.astype(o_ref.dtype)


def _mix1_call(ya, proj, conv_w, wb0, wb1, S, D):
    T, A = ya.shape
    assert conv_w.shape == (CONV_LEN, A) and (6 * A) % D == 0
    tm = min(256, S)
    gcol = 6 * A // D
    hb = tm // SUBLANES
    kern = functools.partial(_mix1_kernel, tm=tm, S=S)
    halo = lambda col: pl.BlockSpec((SUBLANES, A), lambda i: (jnp.maximum(i * hb - 1, 0), col))
    return pl.pallas_call(
        kern,
        grid=(T // tm,),
        in_specs=[
            pl.BlockSpec((tm, A), lambda i: (i, 0)),
            pl.BlockSpec((tm, A), lambda i: (i, 3)),
            pl.BlockSpec((tm, A), lambda i: (i, 4)),
            pl.BlockSpec((tm, A), lambda i: (i, 5)),
            halo(4),
            halo(5),
            pl.BlockSpec((tm, D), lambda i: (i, gcol)),
            pl.BlockSpec((tm, D), lambda i: (i, gcol + 1)),
            pl.BlockSpec((CONV_LEN, A), lambda i: (0, 0)),
            pl.BlockSpec((A, D), lambda i: (0, 0)),
            pl.BlockSpec((A, D), lambda i: (0, 0)),
        ],
        out_specs=pl.BlockSpec((tm, D), lambda i: (i, 0)),
        out_shape=jax.ShapeDtypeStruct((T, D), BF16),
        compiler_params=_cparams(1),
        name="mix_merge",
    )(ya, proj, proj, proj, proj, proj, proj, proj, conv_w, wb0, wb1)


def _mix2_kernel(m_ref, wo_ref, x_ref, ada_ref, g_ref, wr_ref, xo_ref, hp_ref, lg_ref):
    out = jnp.dot(m_ref[...], wo_ref[...], preferred_element_type=F32)
    x = x_ref[...] + ada_ref[2:3, :] * out
    xo_ref[...] = x
    h = _rms_modulate(x, g_ref[...], ada_ref[3:4, :], ada_ref[4:5, :])
    tm, d = x.shape
    _store_rows(hp_ref, _pack_halves(h), tm, d // 2 // LANES)
    e = wr_ref.shape[1] // 2
    h_hi = h.astype(BF16)
    h_lo = (h - h_hi.astype(F32)).astype(BF16)
    lg2 = jnp.dot(h_hi, wr_ref[...], preferred_element_type=F32)
    lg1 = jnp.dot(h_lo, wr_ref[...], preferred_element_type=F32)
    lg = lg2[:, :e] + lg2[:, e:] + lg1[:, :e]
    lg_ref[...] = lg.T


def _mix2_call(merged, w_out, x, ada_l, g, wr, S):
    T, D = x.shape
    tm = min(256, S)
    E = wr.shape[1] // 2
    ch = D // 2 // LANES
    return pl.pallas_call(
        _mix2_kernel,
        grid=(T // tm,),
        in_specs=[
            pl.BlockSpec((tm, D), lambda i: (i, 0)),
            pl.BlockSpec((D, D), lambda i: (0, 0)),
            pl.BlockSpec((tm, D), lambda i: (i, 0)),
            pl.BlockSpec((None, 6, D), lambda i: ((i * tm) // S, 0, 0)),
            pl.BlockSpec((1, D), lambda i: (0, 0)),
            pl.BlockSpec((D, 2 * E), lambda i: (0, 0)),
        ],
        out_specs=[
            pl.BlockSpec((tm, D), lambda i: (i, 0)),
            pl.BlockSpec((tm * ch, LANES), lambda i: (i, 0)),
            pl.BlockSpec((E, tm), lambda i: (0, i)),
        ],
        out_shape=[
            jax.ShapeDtypeStruct((T, D), F32),
            jax.ShapeDtypeStruct((T * ch, LANES), U32),
            jax.ShapeDtypeStruct((E, T), F32),
        ],
        compiler_params=_cparams(1),
        name="outproj_norm_router",
    )(merged, w_out, x, ada_l, g.reshape(1, D), wr)


def _route_kernel(lg_ref, bias_ref, tri_ref, idx_ref, wt_ref, rs_ref, cnt_ref, carry_sc):
    step = pl.program_id(0)

    @pl.when(step == 0)
    def _():
        carry_sc[...] = jnp.zeros_like(carry_sc)

    G = N_GROUPS
    scores = jax.nn.sigmoid(lg_ref[...])
    biased = scores + bias_ref[...]
    sc = [scores[i * G:(i + 1) * G, :] for i in range(GROUP_SIZE)]
    bi = [biased[i * G:(i + 1) * G, :] for i in range(GROUP_SIZE)]
    grow = lax.broadcasted_iota(I32, bi[0].shape, 0)
    eid = [grow * GROUP_SIZE + i for i in range(GROUP_SIZE)]
    neg_inf = jnp.float32(-jnp.inf)

    m1 = functools.reduce(jnp.maximum, bi)
    first = jnp.full(grow.shape, GROUP_SIZE, I32)
    for i in reversed(range(GROUP_SIZE)):
        first = jnp.where(bi[i] == m1, i, first)
    m2 = functools.reduce(jnp.maximum,
                          [jnp.where(first == i, neg_inf, bi[i]) for i in range(GROUP_SIZE)])
    gs = m1 + m2

    grank = jnp.zeros(gs.shape, F32)
    for gp in range(G):
        v = gs[gp:gp + 1, :]
        grank = grank + ((v > gs) | ((v == gs) & (gp < grow))).astype(F32)
    gsel = grank < TOPK_GROUPS

    ms = [jnp.where(gsel, b, neg_inf) for b in bi]
    chosen = [jnp.zeros(grow.shape, jnp.bool_) for _ in range(GROUP_SIZE)]
    picked_idx = []
    picked_score = []
    for _ in range(MOE_TOP_K):
        mx = jnp.max(functools.reduce(jnp.maximum, ms), axis=0, keepdims=True)
        cand = functools.reduce(
            jnp.minimum, [jnp.where(ms[i] == mx, eid[i], N_EXPERTS) for i in range(GROUP_SIZE)])
        ce = jnp.min(cand, axis=0, keepdims=True)
        hit = [eid[i] == ce for i in range(GROUP_SIZE)]
        s_k = jnp.sum(functools.reduce(
            jnp.add, [jnp.where(hit[i], sc[i], 0.0) for i in range(GROUP_SIZE)]),
            axis=0, keepdims=True)
        ms = [jnp.where(hit[i], neg_inf, ms[i]) for i in range(GROUP_SIZE)]
        chosen = [chosen[i] | hit[i] for i in range(GROUP_SIZE)]
        picked_idx.append(ce)
        picked_score.append(s_k)

    wsum = functools.reduce(jnp.add, picked_score)
    for k in range(MOE_TOP_K):
        idx_ref[k:k + 1, :] = picked_idx[k]
        wt_ref[k:k + 1, :] = picked_score[k] / wsum * ROUTED_SCALE

    selm = jnp.concatenate([c.astype(F32) for c in chosen], axis=0)
    pre = jnp.dot(selm.astype(BF16), tri_ref[...], preferred_element_type=F32) + carry_sc[:, 0:1]
    pre_slab = [pre[i * G:(i + 1) * G, :] for i in range(GROUP_SIZE)]
    for k in range(MOE_TOP_K):
        rk = jnp.sum(functools.reduce(
            jnp.add, [jnp.where(eid[i] == picked_idx[k], pre_slab[i], 0.0)
                      for i in range(GROUP_SIZE)]), axis=0, keepdims=True)
        rs_ref[k:k + 1, :] = rk.astype(I32)
    carry_sc[...] = carry_sc[...] + jnp.sum(selm, axis=1, keepdims=True)
    cnt_ref[...] = carry_sc[...]


def _route_call(lgt, bias_perm, T):
    E = lgt.shape[0]
    tr = min(512, T)
    tri = (lax.broadcasted_iota(I32, (tr, tr), 0) < lax.broadcasted_iota(I32, (tr, tr), 1)).astype(BF16)
    return pl.pallas_call(
        _route_kernel,
        grid=(T // tr,),
        in_specs=[
            pl.BlockSpec((E, tr), lambda i: (0, i)),
            pl.BlockSpec((E, 1), lambda i: (0, 0)),
            pl.BlockSpec((tr, tr), lambda i: (0, 0)),
        ],
        out_specs=[
            pl.BlockSpec((MOE_TOP_K, tr), lambda i: (0, i)),
            pl.BlockSpec((MOE_TOP_K, tr), lambda i: (0, i)),
            pl.BlockSpec((MOE_TOP_K, tr), lambda i: (0, i)),
            pl.BlockSpec((E, 128), lambda i: (0, 0)),
        ],
        out_shape=[
            jax.ShapeDtypeStruct((MOE_TOP_K, T), I32),
            jax.ShapeDtypeStruct((MOE_TOP_K, T), F32),
            jax.ShapeDtypeStruct((MOE_TOP_K, T), I32),
            jax.ShapeDtypeStruct((E, 128), F32),
        ],
        scratch_shapes=[pltpu.VMEM((E, 128), F32)],
        compiler_params=_cparams(1),
        name="route",
    )(lgt, bias_perm.reshape(E, 1), tri)


def _dispatch_kernel(zs_ref, nu_ref, hp_ref, dest_ref, xs_ref, z_sc, sem, *, td, ch, n_blocks):
    i = pl.program_id(0)
    zrows = EXPERT_ROWS * ch

    @pl.when(i == 0)
    def _():
        z_sc[...] = jnp.zeros_like(z_sc)
        zero_wait = pltpu.make_async_copy(z_sc, xs_ref.at[pl.ds(0, zrows), :], sem)

        def zero_block(b, carry):
            b0 = pl.multiple_of(b * zrows, zrows)
            pltpu.make_async_copy(z_sc, xs_ref.at[pl.ds(b0, zrows), :], sem).start()
            return carry

        def wait_block(b, carry):
            zero_wait.wait()
            return carry

        for e in range(N_EXPERTS):
            @pl.when(zs_ref[e] >= 0)
            def _(e=e):
                zero_block(zs_ref[e], 0)
        lax.fori_loop(nu_ref[0], n_blocks, zero_block, 0)
        for e in range(N_EXPERTS):
            @pl.when(zs_ref[e] >= 0)
            def _():
                zero_wait.wait()
        lax.fori_loop(nu_ref[0], n_blocks, wait_block, 0)

    def issue(t, carry):
        src = hp_ref.at[pl.ds(pl.multiple_of(t * ch, ch), ch), :]
        for k in range(MOE_TOP_K):
            r0 = pl.multiple_of(dest_ref[t * MOE_TOP_K + k] * ch, ch)
            pltpu.make_async_copy(src, xs_ref.at[pl.ds(r0, ch), :], sem).start(priority=k % 2)
        return carry

    lax.fori_loop(0, td, issue, 0)
    for k in range(MOE_TOP_K):
        pltpu.make_async_copy(hp_ref, xs_ref.at[pl.ds(0, td * ch), :], sem).wait()


def _dispatch_call(hp, dest_flat, zstart, n_used, n_blocks, ch):
    T = hp.shape[0] // ch
    td = min(256, T)
    n_rows = n_blocks * EXPERT_ROWS
    kern = functools.partial(_dispatch_kernel, td=td, ch=ch, n_blocks=n_blocks)
    return pl.pallas_call(
        kern,
        grid_spec=pltpu.PrefetchScalarGridSpec(
            num_scalar_prefetch=2,
            grid=(T // td,),
            in_specs=[
                pl.BlockSpec((td * ch, LANES), lambda i, zs, nu: (i, 0)),
                pl.BlockSpec((td * MOE_TOP_K,), lambda i, zs, nu: (i,), memory_space=pltpu.SMEM),
            ],
            out_specs=pl.BlockSpec(memory_space=pl.ANY),
            scratch_shapes=[pltpu.VMEM((EXPERT_ROWS * ch, LANES), U32), pltpu.SemaphoreType.DMA],
        ),
        out_shape=jax.ShapeDtypeStruct((n_rows * ch, LANES), U32),
        compiler_params=_cparams(1),
        name="dispatch",
    )(zstart, n_used, hp, dest_flat)


def _expert_kernel(be_ref, nu_ref, nx_ref, sl_ref, xs_ref, wg_hbm, wu_hbm, wd_hbm, o_ref,
                   wg_st, wu_st, wd_st, wg_sc, wu_sc, wd_sc, sems, *, layer):
    i = pl.program_id(0)
    e = be_ref[i]
    prev = be_ref[jnp.maximum(i - 1, 0)]
    changed = (i == 0) | (e != prev)

    def weight_copies(ex, slot):
        return (pltpu.make_async_copy(wg_hbm.at[layer, ex], wg_st.at[slot], sems.at[slot, 0]),
                pltpu.make_async_copy(wu_hbm.at[layer, ex], wu_st.at[slot], sems.at[slot, 1]),
                pltpu.make_async_copy(wd_hbm.at[layer, ex], wd_st.at[slot], sems.at[slot, 2]))

    @pl.when(i == 0)
    def _():
        for cp in weight_copies(e, sl_ref[e]):
            cp.start(priority=1)

    @pl.when(changed)
    def _():
        slot = sl_ref[e]
        for cp in weight_copies(e, slot):
            cp.wait()
        wg_sc[...] = wg_st[slot].astype(BF16)
        wu_sc[...] = wu_st[slot].astype(BF16)
        wd_sc[...] = wd_st[slot].astype(BF16)
        nxt = nx_ref[e]

        @pl.when(nxt >= 0)
        def _():
            for cp in weight_copies(nxt, 1 - slot):
                cp.start(priority=1)

    @pl.when(i < nu_ref[0])
    def _():
        half = wg_sc.shape[0] // 2
        ch = half // LANES
        w = _load_rows(xs_ref, EXPERT_ROWS, ch)
        xlo = _unpack_lo(w).astype(BF16)
        xhi = _unpack_hi(w).astype(BF16)
        gate = (jnp.dot(xlo, wg_sc[:half, :], preferred_element_type=F32)
                + jnp.dot(xhi, wg_sc[half:, :], preferred_element_type=F32))
        up = (jnp.dot(xlo, wu_sc[:half, :], preferred_element_type=F32)
              + jnp.dot(xhi, wu_sc[half:, :], preferred_element_type=F32))
        act = (_silu(gate) * up).astype(BF16)
        out = jnp.dot(act, wd_sc[...], preferred_element_type=F32)
        _store_rows(o_ref, _pack_halves(out), EXPERT_ROWS, ch)

    @pl.when(i >= nu_ref[0])
    def _():
        o_ref[...] = jnp.zeros_like(o_ref)


def _expert_call(xs, block_e, n_used, next_e, slot_e, w_gate, w_up, w_down, layer, n_blocks):
    _, E, D, F = w_gate.shape
    ch = D // 2 // LANES
    rows = lambda i, be, nu, nx, sl: (jnp.minimum(i, nu[0] - 1), 0)
    out_rows = lambda i, be, nu, nx, sl: (i, 0)
    kern = functools.partial(_expert_kernel, layer=layer)
    return pl.pallas_call(
        kern,
        grid_spec=pltpu.PrefetchScalarGridSpec(
            num_scalar_prefetch=4,
            grid=(n_blocks,),
            in_specs=[
                pl.BlockSpec((EXPERT_ROWS * ch, LANES), rows),
                pl.BlockSpec(memory_space=pl.ANY),
                pl.BlockSpec(memory_space=pl.ANY),
                pl.BlockSpec(memory_space=pl.ANY),
            ],
            out_specs=pl.BlockSpec((EXPERT_ROWS * ch, LANES), out_rows),
            scratch_shapes=[
                pltpu.VMEM((2, D, F), F32), pltpu.VMEM((2, D, F), F32), pltpu.VMEM((2, F, D), F32),
                pltpu.VMEM((D, F), BF16), pltpu.VMEM((D, F), BF16), pltpu.VMEM((F, D), BF16),
                pltpu.SemaphoreType.DMA((2, 3)),
            ],
        ),
        out_shape=jax.ShapeDtypeStruct((n_blocks * EXPERT_ROWS * ch, LANES), U32),
        compiler_params=_cparams(1),
        name="experts",
    )(block_e, n_used, next_e, slot_e, xs, w_gate, w_up, w_down)


def _combine_kernel(dest_ref, wt_ref, hp_ref, x_ref, ada_ref, wsg_ref, wsu_ref, wsd_ref, fn_ref,
                    eo_ref, xo_ref, gbuf, sem, *, tc, final):
    half = x_ref.shape[1] // 2
    ch = half // LANES

    def issue(t, carry):
        t0 = pl.multiple_of(t * ch, ch)
        for k in range(MOE_TOP_K):
            r0 = pl.multiple_of(dest_ref[t * MOE_TOP_K + k] * ch, ch)
            pltpu.make_async_copy(eo_ref.at[pl.ds(r0, ch), :], gbuf.at[k, pl.ds(t0, ch), :],
                                  sem).start(priority=k % 2)
        return carry

    lax.fori_loop(0, tc, issue, 0)

    w = _load_rows(hp_ref, tc, ch)
    hlo = _unpack_lo(w).astype(BF16)
    hhi = _unpack_hi(w).astype(BF16)
    gate = (jnp.dot(hlo, wsg_ref[:half, :], preferred_element_type=F32)
            + jnp.dot(hhi, wsg_ref[half:, :], preferred_element_type=F32))
    up = (jnp.dot(hlo, wsu_ref[:half, :], preferred_element_type=F32)
          + jnp.dot(hhi, wsu_ref[half:, :], preferred_element_type=F32))
    act = (_silu(gate) * up).astype(BF16)
    shared = jnp.dot(act, wsd_ref[...], preferred_element_type=F32)

    for k in range(MOE_TOP_K):
        pltpu.make_async_copy(eo_ref.at[pl.ds(0, tc * ch), :], gbuf.at[k], sem).wait()

    wts = wt_ref[...]
    ylo = shared[:, :half]
    yhi = shared[:, half:]
    for k in range(MOE_TOP_K):
        g = _load_rows(gbuf.at[k], tc, ch)
        wk = wts[:, k:k + 1]
        ylo = ylo + wk * _unpack_lo(g)
        yhi = yhi + wk * _unpack_hi(g)
    gf = ada_ref[5:6, :]
    xlo = x_ref[:, :half] + gf[:, :half] * ylo
    xhi = x_ref[:, half:] + gf[:, half:] * yhi
    if final:
        ss = (jnp.sum(xlo * xlo, axis=-1, keepdims=True)
              + jnp.sum(xhi * xhi, axis=-1, keepdims=True))
        inv = lax.rsqrt(ss / (2 * half) + NORM_EPS)
        xlo = xlo * inv * fn_ref[:, :half]
        xhi = xhi * inv * fn_ref[:, half:]
    xo_ref[:, :half] = xlo
    xo_ref[:, half:] = xhi


def _combine_call(dest_flat, wts, hp, x, ada_l, wsg, wsu, wsd, fnorm, eo, S, final):
    T, D = x.shape
    ch = D // 2 // LANES
    Fs = wsg.shape[1]
    tc = min(128, S)
    kern = functools.partial(_combine_kernel, tc=tc, final=final)
    return pl.pallas_call(
        kern,
        grid=(T // tc,),
        in_specs=[
            pl.BlockSpec((tc * MOE_TOP_K,), lambda i: (i,), memory_space=pltpu.SMEM),
            pl.BlockSpec((tc, MOE_TOP_K), lambda i: (i, 0)),
            pl.BlockSpec((tc * ch, LANES), lambda i: (i, 0)),
            pl.BlockSpec((tc, D), lambda i: (i, 0)),
            pl.BlockSpec((None, 6, D), lambda i: ((i * tc) // S, 0, 0)),
            pl.BlockSpec((D, Fs), lambda i: (0, 0)),
            pl.BlockSpec((D, Fs), lambda i: (0, 0)),
            pl.BlockSpec((Fs, D), lambda i: (0, 0)),
            pl.BlockSpec((1, D), lambda i: (0, 0)),
            pl.BlockSpec(memory_space=pl.ANY),
        ],
        out_specs=pl.BlockSpec((tc, D), lambda i: (i, 0)),
        out_shape=jax.ShapeDtypeStruct((T, D), F32),
        scratch_shapes=[pltpu.VMEM((MOE_TOP_K, tc * ch, LANES), U32), pltpu.SemaphoreType.DMA],
        compiler_params=_cparams(1),
        name="combine_shared",
    )(dest_flat, wts, hp, x, ada_l, wsg, wsu, wsd, fnorm.reshape(1, D), eo)


def _rope_tables(positions):
    half = ROPE_DIM // 2
    inv_freq = ROPE_THETA ** (-jnp.arange(0, ROPE_DIM, 2, dtype=F32) / ROPE_DIM)
    ang = positions.astype(F32).reshape(-1, 1) * inv_freq
    cos, sin = jnp.cos(ang), jnp.sin(ang)
    T = ang.shape[0]
    rest = HEAD_DIM - ROPE_DIM
    c = jnp.concatenate([cos, cos, jnp.ones((T, rest), F32)], axis=-1)
    s = jnp.concatenate([-sin, sin, jnp.zeros((T, rest), F32)], axis=-1)
    return c, s


def _expert_row_perm():
    r = jnp.arange(N_EXPERTS)
    return (r % N_GROUPS) * GROUP_SIZE + r // N_GROUPS


def kernel(x, c, positions, norm_mix, w_ada, b_ada, w_in, conv_w, w_branch, w_out, norm_ffn, w_router, router_bias, w_gate, w_up, w_down, ws_gate, ws_up, ws_down, final_norm):
    B, S, D = x.shape
    L = w_ada.shape[0]
    T = B * S
    A = w_branch.shape[2]
    H = A // HEAD_DIM
    assert S % MOBA_BLOCK == 0 and w_in.shape[2] == 6 * A + 2 * D

    n_assign = T * MOE_TOP_K
    n_blocks = -(-(n_assign + N_EXPERTS * (EXPERT_ROWS - 1)) // EXPERT_ROWS)

    rope_c, rope_s = _rope_tables(positions)
    ada = _ada_call(c, w_ada, b_ada).reshape(L, B, 6, D)
    perm = _expert_row_perm()
    xf = x.reshape(T, D)

    for l in range(L):
        proj = _inproj_call(xf, norm_mix[l], ada[l], w_in[l].astype(BF16), S)
        ya = _attn_call(proj, rope_c, rope_s, B, S, H)
        merged = _mix1_call(ya, proj, conv_w[l], w_branch[l, 0].astype(BF16),
                            w_branch[l, 1].astype(BF16), S, D)
        wr = w_router[l][:, perm]
        wr_hi = wr.astype(BF16)
        wr_lo = (wr - wr_hi.astype(F32)).astype(BF16)
        xf, hp, lgt = _mix2_call(merged, w_out[l].astype(BF16), xf, ada[l], norm_ffn[l],
                                 jnp.concatenate([wr_hi, wr_lo], axis=1), S)

        idx_t, wt_t, rs_t, cnt = _route_call(lgt, router_bias[l][perm], T)
        counts = cnt[perm, 0].astype(I32)
        padded = (counts + EXPERT_ROWS - 1) // EXPERT_ROWS * EXPERT_ROWS
        pad_ends = jnp.cumsum(padded)
        pad_starts = pad_ends - padded
        seg_start = jnp.sum(jnp.where(idx_t[:, :, None] == jnp.arange(N_EXPERTS), pad_starts, 0),
                            axis=-1)
        dest_flat = (seg_start + rs_t).T.reshape(-1)
        n_used = (pad_ends[-1] // EXPERT_ROWS).astype(I32)
        blk = jnp.minimum(jnp.arange(n_blocks, dtype=I32), n_used - 1)
        block_e = jnp.minimum(
            jnp.sum(pad_ends[None, :] <= (blk * EXPERT_ROWS)[:, None], axis=1),
            N_EXPERTS - 1).astype(I32)

        n_used = n_used.reshape(1)
        last_blk = jnp.where(padded > 0, pad_ends // EXPERT_ROWS - 1, -1).astype(I32)
        xs = _dispatch_call(hp, dest_flat, last_blk, n_used, n_blocks, D // 2 // LANES)
        present = padded > 0
        eids = jnp.arange(N_EXPERTS, dtype=I32)
        suffix_min = lax.cummin(jnp.where(present, eids, N_EXPERTS)[::-1])[::-1]
        next_e = jnp.concatenate([suffix_min[1:], jnp.full((1,), N_EXPERTS, I32)])
        next_e = jnp.where(next_e >= N_EXPERTS, -1, next_e).astype(I32)
        slot_e = ((jnp.cumsum(present) - present) % 2).astype(I32)
        eo = _expert_call(xs, block_e, n_used, next_e, slot_e, w_gate, w_up, w_down, l, n_blocks)
        xf = _combine_call(dest_flat, wt_t.T, hp, xf, ada[l], ws_gate[l].astype(BF16),
                           ws_up[l].astype(BF16), ws_down[l].astype(BF16), final_norm, eo, S,
                           final=(l == L - 1))
    return xf.reshape(B, S, D)
```

```python
import functools

import jax
import jax.numpy as jnp
from jax import lax
from jax.experimental import pallas as pl
from jax.experimental.pallas import tpu as pltpu

F32 = jnp.float32
BF16 = jnp.bfloat16
U32 = jnp.uint32
I32 = jnp.int32

HEAD_DIM = 128
ROPE_DIM = HEAD_DIM // 4
ROPE_THETA = 500000.0
MOBA_BLOCK = 256
MOBA_TOPK = 3
CONV_LEN = 3
N_EXPERTS = 64
MOE_TOP_K = 8
N_GROUPS = 8
GROUP_SIZE = N_EXPERTS // N_GROUPS
TOPK_GROUPS = 4
ROUTED_SCALE = 2.5
NORM_EPS = 1e-6

VMEM_LIMIT_BYTES = 56 * 1024 * 1024
SUBLANES = 8
EXPERT_ROWS = 256
EXPERT_SUBBLOCKS = 1
ATTN_HEAD_GROUP = 4
ISSUE_UNROLL = 4
NEG_BIG = -1e30
LOG2_E = 1.4426950408889634
SUM_ROWS = 16

NT_DIMS = (((1,), (1,)), ((), ()))


def _cparams(n_axes):
    return pltpu.CompilerParams(
        dimension_semantics=("arbitrary",) * n_axes,
        vmem_limit_bytes=VMEM_LIMIT_BYTES)


def _bits(v):
    return lax.bitcast_convert_type(v, U32)


def _pack_halves(v):
    n = v.shape[1] // 2
    lo = _bits(v[:, :n].astype(BF16).astype(F32))
    hi = _bits(v[:, n:].astype(BF16).astype(F32))
    return hi | (lo >> 16)


def _unpack_lo(w):
    return lax.bitcast_convert_type(w << 16, F32)


def _unpack_hi(w):
    return lax.bitcast_convert_type(w & jnp.uint32(0xFFFF0000), F32)


LANES = 128


def _load_rows(ref, n, ch):
    return jnp.concatenate([ref[pl.ds(s, n, stride=ch), :] for s in range(ch)], axis=1)


def _store_rows(ref, val, n, ch):
    for s in range(ch):
        ref[pl.ds(s, n, stride=ch), :] = val[:, s * LANES:(s + 1) * LANES]


def _silu(v):
    return v * jax.nn.sigmoid(v)


def _rms_modulate(x, g, shift, scale):
    y = x * lax.rsqrt(jnp.mean(x * x, axis=-1, keepdims=True) + NORM_EPS)
    return (y * g) * (1.0 + scale) + shift


def _ada_kernel(c_ref, w_ref, b_ref, o_ref):
    ca = _silu(c_ref[...]).astype(BF16)
    o_ref[0] = jnp.dot(ca, w_ref[0].astype(BF16), preferred_element_type=F32) + b_ref[0]


def _ada_call(c, w_ada, b_ada):
    L, D, N = w_ada.shape
    B = c.shape[0]
    tn = min(512, N)
    return pl.pallas_call(
        _ada_kernel,
        grid=(L, N // tn),
        in_specs=[
            pl.BlockSpec((B, D), lambda l, j: (0, 0)),
            pl.BlockSpec((1, D, tn), lambda l, j: (l, 0, j)),
            pl.BlockSpec((1, 1, tn), lambda l, j: (l, 0, j)),
        ],
        out_specs=pl.BlockSpec((1, B, tn), lambda l, j: (l, 0, j)),
        out_shape=jax.ShapeDtypeStruct((L, B, N), F32),
        compiler_params=_cparams(2),
        name="ada",
    )(c, w_ada, b_ada.reshape(L, 1, N))


def _inproj_kernel(x_ref, g_ref, ada_ref, w_ref, o_ref, h_sc):
    @pl.when(pl.program_id(1) == 0)
    def _():
        h = _rms_modulate(x_ref[...], g_ref[...], ada_ref[0:1, :], ada_ref[1:2, :])
        h_sc[...] = h.astype(BF16)

    o_ref[...] = jnp.dot(h_sc[...], w_ref[...], preferred_element_type=F32).astype(o_ref.dtype)


def _inproj_call(x, g, ada_l, w, S):
    T, D = x.shape
    N = w.shape[1]
    tm = min(1024, S)
    tn = min(512, N)
    return pl.pallas_call(
        _inproj_kernel,
        grid=(T // tm, N // tn),
        in_specs=[
            pl.BlockSpec((tm, D), lambda i, j: (i, 0)),
            pl.BlockSpec((1, D), lambda i, j: (0, 0)),
            pl.BlockSpec((None, 6, D), lambda i, j: ((i * tm) // S, 0, 0)),
            pl.BlockSpec((D, tn), lambda i, j: (0, j)),
        ],
        out_specs=pl.BlockSpec((tm, tn), lambda i, j: (i, j)),
        out_shape=jax.ShapeDtypeStruct((T, N), BF16),
        scratch_shapes=[pltpu.VMEM((tm, D), BF16)],
        compiler_params=_cparams(2),
        name="inproj",
    )(x, g.reshape(1, D), ada_l, w)


def _rope(v, c, s):
    half = ROPE_DIM // 2
    lane = lax.broadcasted_iota(I32, v.shape, 1)
    partner = jnp.where(lane < half,
                        pltpu.roll(v, HEAD_DIM - half, axis=1),
                        pltpu.roll(v, half, axis=1))
    return v * c + partner * s


def _attn_kernel(q_ref, k_ref, v_ref, cq_ref, sq_ref, ck_ref, sk_ref, o_ref,
                 kr_sc, vt_sc, km_sc, sel_sc, qs_sc, s_sc, s2_sc, m_sc, acc_sc, *, nb, hg):
    qi = pl.program_id(2)
    blk = MOBA_BLOCK
    dh = HEAD_DIM
    cols = lambda h: slice(h * dh, (h + 1) * dh)

    @pl.when(qi == 0)
    def _():
        km_sc[...] = jnp.zeros_like(km_sc)
        for j in range(nb):
            rows = pl.ds(j * blk, blk)
            ck = ck_ref[rows, :]
            sk = sk_ref[rows, :]
            for h in range(hg):
                kr = _rope(k_ref[rows, cols(h)].astype(F32), ck, sk)
                kr_sc[h, j] = kr.astype(BF16)
                km_sc[h, j:j + 1, :] = jnp.mean(kr, axis=0, keepdims=True)
                vt = v_ref[rows, cols(h)].astype(F32).T.astype(BF16)
                vt_sc[h, j] = jnp.concatenate([vt, jnp.ones((SUM_ROWS, blk), BF16)], axis=0)

    cq = cq_ref[...]
    sq = sq_ref[...]
    kidx = lax.broadcasted_iota(I32, (blk, blk), 0)
    qidx = lax.broadcasted_iota(I32, (blk, blk), 1)
    causal = kidx <= qidx
    for h in range(hg):
        q = _rope(q_ref[:, cols(h)].astype(F32), cq, sq)
        sel_sc[h] = lax.dot_general(km_sc[h], q, NT_DIMS, precision=lax.Precision.HIGHEST,
                                    preferred_element_type=F32)
        qs = (q * (HEAD_DIM ** -0.5 * LOG2_E)).astype(BF16)
        qs_sc[h] = qs
        st = lax.dot_general(kr_sc[h, qi], qs, NT_DIMS, preferred_element_type=F32)
        s_sc[h] = jnp.where(causal, st, NEG_BIG)

    for h in range(hg):
        gt = sel_sc[h]
        row = lax.broadcasted_iota(I32, gt.shape, 0)
        past = row < qi
        gm = jnp.where(past, gt, -jnp.inf)
        rank = jnp.zeros(gt.shape, F32)
        for jp in range(nb):
            gj = gm[jp:jp + 1, :]
            beats = (gj > gm) | ((gj == gm) & (jp < row))
            rank = rank + beats.astype(F32)
        sel_sc[h] = (past & (rank < MOBA_TOPK)).astype(F32)

    for h in range(hg):
        st = s_sc[h]
        m0 = jnp.max(st, axis=0, keepdims=True)
        p0 = jnp.exp2(st - m0)
        m_sc[h] = m0
        acc_sc[h] = jnp.dot(vt_sc[h, qi], p0.astype(BF16), preferred_element_type=F32)

    def stage_scores(j, buf):
        jc = jnp.minimum(j, nb - 1)
        for h in range(hg):
            buf[h] = lax.dot_general(kr_sc[h, jc], qs_sc[h], NT_DIMS, preferred_element_type=F32)

    def consume(j, buf):
        jc = jnp.minimum(j, nb - 1)
        for h in range(hg):
            sj = jnp.where(sel_sc[h, pl.ds(jc, 1), :] > 0.0, buf[h], NEG_BIG)
            m_old = m_sc[h]
            m_new = jnp.maximum(m_old, jnp.max(sj, axis=0, keepdims=True))
            alpha = jnp.exp2(m_old - m_new)
            p = jnp.exp2(sj - m_new)
            acc_sc[h] = alpha * acc_sc[h] + jnp.dot(vt_sc[h, jc], p.astype(BF16),
                                                    preferred_element_type=F32)
            m_sc[h] = m_new

    stage_scores(0, s_sc)

    def past_pair(jj, carry):
        a = 2 * jj
        stage_scores(a + 1, s2_sc)
        consume(a, s_sc)
        stage_scores(a + 2, s_sc)
        consume(a + 1, s2_sc)
        return carry

    lax.fori_loop(0, (qi + 1) // 2, past_pair, 0)

    for h in range(hg):
        o = acc_sc[h, :dh, :] / acc_sc[h, dh:dh + 1, :]
        o_ref[:, cols(h)] = o.T.astype(o_ref.dtype)


def _attn_call(proj, rope_c, rope_s, B, S, H):
    T = B * S
    blk = MOBA_BLOCK
    nb = S // blk
    nbp = -(-nb // SUBLANES) * SUBLANES
    dh = HEAD_DIM
    hg = min(ATTN_HEAD_GROUP, H)
    ng = H // hg
    w = hg * dh
    kern = functools.partial(_attn_kernel, nb=nb, hg=hg)
    return pl.pallas_call(
        kern,
        grid=(B, ng, nb),
        in_specs=[
            pl.BlockSpec((blk, w), lambda b, g, i: (b * nb + i, g)),
            pl.BlockSpec((S, w), lambda b, g, i: (b, ng + g)),
            pl.BlockSpec((S, w), lambda b, g, i: (b, 2 * ng + g)),
            pl.BlockSpec((blk, dh), lambda b, g, i: (b * nb + i, 0)),
            pl.BlockSpec((blk, dh), lambda b, g, i: (b * nb + i, 0)),
            pl.BlockSpec((S, dh), lambda b, g, i: (b, 0)),
            pl.BlockSpec((S, dh), lambda b, g, i: (b, 0)),
        ],
        out_specs=pl.BlockSpec((blk, w), lambda b, g, i: (b * nb + i, g)),
        out_shape=jax.ShapeDtypeStruct((T, H * dh), BF16),
        scratch_shapes=[
            pltpu.VMEM((hg, nb, blk, dh), BF16),
            pltpu.VMEM((hg, nb, dh + SUM_ROWS, blk), BF16),
            pltpu.VMEM((hg, nbp, dh), F32),
            pltpu.VMEM((hg, nbp, blk), F32),
            pltpu.VMEM((hg, blk, dh), BF16),
            pltpu.VMEM((hg, blk, blk), F32),
            pltpu.VMEM((hg, blk, blk), F32),
            pltpu.VMEM((hg, 1, blk), F32),
            pltpu.VMEM((hg, dh + SUM_ROWS, blk), F32),
        ],
        compiler_params=_cparams(3),
        name="moba_attn",
    )(proj, proj, proj, rope_c, rope_s, rope_c, rope_s)


def _mix1_kernel(ya_ref, cb_ref, cc_ref, cx_ref, hcc_ref, hcx_ref, ga_ref, gc_ref, cw_ref,
                 wb0_ref, wb1_ref, o_ref, *, tm, S):
    i = pl.program_id(0)
    u = cc_ref[...].astype(F32) * cx_ref[...].astype(F32)
    hu = hcc_ref[...].astype(F32) * hcx_ref[...].astype(F32)
    seq_start = (i * tm) % S == 0
    hu = jnp.where(seq_start, 0.0, hu)
    h1 = hu[SUBLANES - 1:SUBLANES, :]
    h2 = hu[SUBLANES - 2:SUBLANES - 1, :]
    r = lax.broadcasted_iota(I32, u.shape, 0)
    u1 = jnp.where(r == 0, h1, pltpu.roll(u, 1, axis=0))
    u2 = jnp.where(r == 0, h2, jnp.where(r == 1, h1, pltpu.roll(u, 2, axis=0)))
    conv = cw_ref[0:1, :] * u2 + cw_ref[1:2, :] * u1 + cw_ref[2:3, :] * u
    yc = (cb_ref[...].astype(F32) * conv).astype(BF16)
    a = jnp.dot(ya_ref[...], wb0_ref[...], preferred_element_type=F32)
    c = jnp.dot(yc, wb1_ref[...], preferred_element_type=F32)
    merged = (jax.nn.sigmoid(ga_ref[...].astype(F32)) * a
              + jax.nn.sigmoid(gc_ref[...].astype(F32)) * c)
    o_ref[...] = merged.astype(o_ref.dtype)


def _mix1_call(ya, proj, conv_w, wb0, wb1, S, D):
    T, A = ya.shape
    assert conv_w.shape == (CONV_LEN, A) and (6 * A) % D == 0
    tm = min(256, S)
    gcol = 6 * A // D
    hb = tm // SUBLANES
    kern = functools.partial(_mix1_kernel, tm=tm, S=S)
    halo = lambda col: pl.BlockSpec((SUBLANES, A), lambda i: (jnp.maximum(i * hb - 1, 0), col))
    return pl.pallas_call(
        kern,
        grid=(T // tm,),
        in_specs=[
            pl.BlockSpec((tm, A), lambda i: (i, 0)),
            pl.BlockSpec((tm, A), lambda i: (i, 3)),
            pl.BlockSpec((tm, A), lambda i: (i, 4)),
            pl.BlockSpec((tm, A), lambda i: (i, 5)),
            halo(4),
            halo(5),
            pl.BlockSpec((tm, D), lambda i: (i, gcol)),
            pl.BlockSpec((tm, D), lambda i: (i, gcol + 1)),
            pl.BlockSpec((CONV_LEN, A), lambda i: (0, 0)),
            pl.BlockSpec((A, D), lambda i: (0, 0)),
            pl.BlockSpec((A, D), lambda i: (0, 0)),
        ],
        out_specs=pl.BlockSpec((tm, D), lambda i: (i, 0)),
        out_shape=jax.ShapeDtypeStruct((T, D), BF16),
        compiler_params=_cparams(1),
        name="mix_merge",
    )(ya, proj, proj, proj, proj, proj, proj, proj, conv_w, wb0, wb1)


def _mix2_kernel(m_ref, wo_ref, x_ref, ada_ref, g_ref, wr_ref, xo_ref, hp_ref, lg_ref):
    out = jnp.dot(m_ref[...], wo_ref[...], preferred_element_type=F32)
    x = x_ref[...] + ada_ref[2:3, :] * out
    xo_ref[...] = x
    h = _rms_modulate(x, g_ref[...], ada_ref[3:4, :], ada_ref[4:5, :])
    tm, d = x.shape
    _store_rows(hp_ref, _pack_halves(h), tm, d // 2 // LANES)
    e = wr_ref.shape[1] // 2
    h_hi = h.astype(BF16)
    h_lo = (h - h_hi.astype(F32)).astype(BF16)
    lg2 = jnp.dot(h_hi, wr_ref[...], preferred_element_type=F32)
    lg1 = jnp.dot(h_lo, wr_ref[...], preferred_element_type=F32)
    lg = lg2[:, :e] + lg2[:, e:] + lg1[:, :e]
    lg_ref[...] = lg.T


def _mix2_call(merged, w_out, x, ada_l, g, wr, S):
    T, D = x.shape
    tm = min(256, S)
    E = wr.shape[1] // 2
    ch = D // 2 // LANES
    return pl.pallas_call(
        _mix2_kernel,
        grid=(T // tm,),
        in_specs=[
            pl.BlockSpec((tm, D), lambda i: (i, 0)),
            pl.BlockSpec((D, D), lambda i: (0, 0)),
            pl.BlockSpec((tm, D), lambda i: (i, 0)),
            pl.BlockSpec((None, 6, D), lambda i: ((i * tm) // S, 0, 0)),
            pl.BlockSpec((1, D), lambda i: (0, 0)),
            pl.BlockSpec((D, 2 * E), lambda i: (0, 0)),
        ],
        out_specs=[
            pl.BlockSpec((tm, D), lambda i: (i, 0)),
            pl.BlockSpec((tm * ch, LANES), lambda i: (i, 0)),
            pl.BlockSpec((E, tm), lambda i: (0, i)),
        ],
        out_shape=[
            jax.ShapeDtypeStruct((T, D), F32),
            jax.ShapeDtypeStruct((T * ch, LANES), U32),
            jax.ShapeDtypeStruct((E, T), F32),
        ],
        compiler_params=_cparams(1),
        name="outproj_norm_router",
    )(merged, w_out, x, ada_l, g.reshape(1, D), wr)


def _route_kernel(lg_ref, bias_ref, tri_ref, idx_ref, wt_ref, rs_ref, cnt_ref, carry_sc):
    step = pl.program_id(0)

    @pl.when(step == 0)
    def _():
        carry_sc[...] = jnp.zeros_like(carry_sc)

    G = N_GROUPS
    scores = jax.nn.sigmoid(lg_ref[...])
    biased = scores + bias_ref[...]
    sc = [scores[i * G:(i + 1) * G, :] for i in range(GROUP_SIZE)]
    bi = [biased[i * G:(i + 1) * G, :] for i in range(GROUP_SIZE)]
    grow = lax.broadcasted_iota(I32, bi[0].shape, 0)
    eid = [grow * GROUP_SIZE + i for i in range(GROUP_SIZE)]
    neg_inf = jnp.float32(-jnp.inf)

    m1 = functools.reduce(jnp.maximum, bi)
    first = jnp.full(grow.shape, GROUP_SIZE, I32)
    for i in reversed(range(GROUP_SIZE)):
        first = jnp.where(bi[i] == m1, i, first)
    m2 = functools.reduce(jnp.maximum,
                          [jnp.where(first == i, neg_inf, bi[i]) for i in range(GROUP_SIZE)])
    gs = m1 + m2

    grank = jnp.zeros(gs.shape, F32)
    for gp in range(G):
        v = gs[gp:gp + 1, :]
        grank = grank + ((v > gs) | ((v == gs) & (gp < grow))).astype(F32)
    gsel = grank < TOPK_GROUPS

    ms = [jnp.where(gsel, b, neg_inf) for b in bi]
    chosen = [jnp.zeros(grow.shape, jnp.bool_) for _ in range(GROUP_SIZE)]
    picked_idx = []
    picked_score = []
    for _ in range(MOE_TOP_K):
        mx = jnp.max(functools.reduce(jnp.maximum, ms), axis=0, keepdims=True)
        cand = functools.reduce(
            jnp.minimum, [jnp.where(ms[i] == mx, eid[i], N_EXPERTS) for i in range(GROUP_SIZE)])
        ce = jnp.min(cand, axis=0, keepdims=True)
        hit = [eid[i] == ce for i in range(GROUP_SIZE)]
        s_k = jnp.sum(functools.reduce(
            jnp.add, [jnp.where(hit[i], sc[i], 0.0) for i in range(GROUP_SIZE)]),
            axis=0, keepdims=True)
        ms = [jnp.where(hit[i], neg_inf, ms[i]) for i in range(GROUP_SIZE)]
        chosen = [chosen[i] | hit[i] for i in range(GROUP_SIZE)]
        picked_idx.append(ce)
        picked_score.append(s_k)

    wsum = functools.reduce(jnp.add, picked_score)
    for k in range(MOE_TOP_K):
        idx_ref[k:k + 1, :] = picked_idx[k]
        wt_ref[k:k + 1, :] = picked_score[k] / wsum * ROUTED_SCALE

    selm = jnp.concatenate([c.astype(F32) for c in chosen], axis=0)
    pre = jnp.dot(selm.astype(BF16), tri_ref[...], preferred_element_type=F32) + carry_sc[:, 0:1]
    pre_slab = [pre[i * G:(i + 1) * G, :] for i in range(GROUP_SIZE)]
    for k in range(MOE_TOP_K):
        rk = jnp.sum(functools.reduce(
            jnp.add, [jnp.where(eid[i] == picked_idx[k], pre_slab[i], 0.0)
                      for i in range(GROUP_SIZE)]), axis=0, keepdims=True)
        rs_ref[k:k + 1, :] = rk.astype(I32)
    carry_sc[...] = carry_sc[...] + jnp.sum(selm, axis=1, keepdims=True)
    cnt_ref[...] = carry_sc[...]


def _route_call(lgt, bias_perm, T):
    E = lgt.shape[0]
    tr = min(512, T)
    tri = (lax.broadcasted_iota(I32, (tr, tr), 0) < lax.broadcasted_iota(I32, (tr, tr), 1)).astype(BF16)
    return pl.pallas_call(
        _route_kernel,
        grid=(T // tr,),
        in_specs=[
            pl.BlockSpec((E, tr), lambda i: (0, i)),
            pl.BlockSpec((E, 1), lambda i: (0, 0)),
            pl.BlockSpec((tr, tr), lambda i: (0, 0)),
        ],
        out_specs=[
            pl.BlockSpec((MOE_TOP_K, tr), lambda i: (0, i)),
            pl.BlockSpec((MOE_TOP_K, tr), lambda i: (0, i)),
            pl.BlockSpec((MOE_TOP_K, tr), lambda i: (0, i)),
            pl.BlockSpec((E, 128), lambda i: (0, 0)),
        ],
        out_shape=[
            jax.ShapeDtypeStruct((MOE_TOP_K, T), I32),
            jax.ShapeDtypeStruct((MOE_TOP_K, T), F32),
            jax.ShapeDtypeStruct((MOE_TOP_K, T), I32),
            jax.ShapeDtypeStruct((E, 128), F32),
        ],
        scratch_shapes=[pltpu.VMEM((E, 128), F32)],
        compiler_params=_cparams(1),
        name="route",
    )(lgt, bias_perm.reshape(E, 1), tri)


def _dispatch_kernel(zs_ref, nu_ref, hp_ref, dest_ref, xs_ref, z_sc, sem, *, td, ch, n_blocks):
    i = pl.program_id(0)
    zrows = EXPERT_ROWS * ch

    @pl.when(i == 0)
    def _():
        z_sc[...] = jnp.zeros_like(z_sc)
        zero_wait = pltpu.make_async_copy(z_sc, xs_ref.at[pl.ds(0, zrows), :], sem)

        def zero_block(b, carry):
            b0 = pl.multiple_of(b * zrows, zrows)
            pltpu.make_async_copy(z_sc, xs_ref.at[pl.ds(b0, zrows), :], sem).start()
            return carry

        def wait_block(b, carry):
            zero_wait.wait()
            return carry

        for e in range(N_EXPERTS):
            @pl.when(zs_ref[e] >= 0)
            def _(e=e):
                zero_block(zs_ref[e], 0)
        lax.fori_loop(nu_ref[0], n_blocks, zero_block, 0)
        for e in range(N_EXPERTS):
            @pl.when(zs_ref[e] >= 0)
            def _():
                zero_wait.wait()
        lax.fori_loop(nu_ref[0], n_blocks, wait_block, 0)

    def issue(tp, carry):
        for u in range(ISSUE_UNROLL):
            t = tp * ISSUE_UNROLL + u
            src = hp_ref.at[pl.ds(pl.multiple_of(t * ch, ch), ch), :]
            for k in range(MOE_TOP_K):
                r0 = pl.multiple_of(dest_ref[t * MOE_TOP_K + k] * ch, ch)
                pltpu.make_async_copy(src, xs_ref.at[pl.ds(r0, ch), :], sem).start(priority=k % 2)
        return carry

    lax.fori_loop(0, td // ISSUE_UNROLL, issue, 0)
    for k in range(MOE_TOP_K):
        pltpu.make_async_copy(hp_ref, xs_ref.at[pl.ds(0, td * ch), :], sem).wait()


def _dispatch_call(hp, dest_flat, zstart, n_used, n_blocks, ch):
    T = hp.shape[0] // ch
    td = min(256, T)
    n_rows = n_blocks * EXPERT_ROWS
    kern = functools.partial(_dispatch_kernel, td=td, ch=ch, n_blocks=n_blocks)
    return pl.pallas_call(
        kern,
        grid_spec=pltpu.PrefetchScalarGridSpec(
            num_scalar_prefetch=2,
            grid=(T // td,),
            in_specs=[
                pl.BlockSpec((td * ch, LANES), lambda i, zs, nu: (i, 0)),
                pl.BlockSpec((td * MOE_TOP_K,), lambda i, zs, nu: (i,), memory_space=pltpu.SMEM),
            ],
            out_specs=pl.BlockSpec(memory_space=pl.ANY),
            scratch_shapes=[pltpu.VMEM((EXPERT_ROWS * ch, LANES), U32), pltpu.SemaphoreType.DMA],
        ),
        out_shape=jax.ShapeDtypeStruct((n_rows * ch, LANES), U32),
        compiler_params=_cparams(1),
        name="dispatch",
    )(zstart, n_used, hp, dest_flat)


def _expert_kernel(be_ref, nu_ref, nx_ref, sl_ref, xs_ref, wg_hbm, wu_hbm, wd_hbm, o_ref,
                   wg_st, wu_st, wd_st, wg_sc, wu_sc, wd_sc, sems, *, layer):
    i = pl.program_id(0)
    e = be_ref[i]
    prev = be_ref[jnp.maximum(i - 1, 0)]
    changed = (i == 0) | (e != prev)

    def weight_copies(ex, slot):
        return (pltpu.make_async_copy(wg_hbm.at[layer, ex], wg_st.at[slot], sems.at[slot, 0]),
                pltpu.make_async_copy(wu_hbm.at[layer, ex], wu_st.at[slot], sems.at[slot, 1]),
                pltpu.make_async_copy(wd_hbm.at[layer, ex], wd_st.at[slot], sems.at[slot, 2]))

    @pl.when(i == 0)
    def _():
        for cp in weight_copies(e, sl_ref[e]):
            cp.start(priority=1)

    @pl.when(changed)
    def _():
        slot = sl_ref[e]
        for cp in weight_copies(e, slot):
            cp.wait()
        wg_sc[...] = wg_st[slot].astype(BF16)
        wu_sc[...] = wu_st[slot].astype(BF16)
        wd_sc[...] = wd_st[slot].astype(BF16)
        nxt = nx_ref[e]

        @pl.when(nxt >= 0)
        def _():
            for cp in weight_copies(nxt, 1 - slot):
                cp.start(priority=1)

    @pl.when(i < nu_ref[0])
    def _():
        ch = wg_sc.shape[0] // 2 // LANES
        sub = EXPERT_ROWS // EXPERT_SUBBLOCKS
        gu = []
        for s in range(EXPERT_SUBBLOCKS):
            w = _load_rows(xs_ref.at[pl.ds(s * sub * ch, sub * ch), :], sub, ch)
            x = jnp.concatenate([_unpack_lo(w).astype(BF16), _unpack_hi(w).astype(BF16)], axis=1)
            gu.append((jnp.dot(x, wg_sc[...], preferred_element_type=F32),
                       jnp.dot(x, wu_sc[...], preferred_element_type=F32)))
        for s in range(EXPERT_SUBBLOCKS):
            gate, up = gu[s]
            act = (_silu(gate) * up).astype(BF16)
            out = jnp.dot(act, wd_sc[...], preferred_element_type=F32)
            _store_rows(o_ref.at[pl.ds(s * sub * ch, sub * ch), :], _pack_halves(out), sub, ch)

    @pl.when(i >= nu_ref[0])
    def _():
        o_ref[...] = jnp.zeros_like(o_ref)


def _expert_call(xs, block_e, n_used, next_e, slot_e, w_gate, w_up, w_down, layer, n_blocks):
    _, E, D, F = w_gate.shape
    ch = D // 2 // LANES
    rows = lambda i, be, nu, nx, sl: (jnp.minimum(i, nu[0] - 1), 0)
    out_rows = lambda i, be, nu, nx, sl: (i, 0)
    kern = functools.partial(_expert_kernel, layer=layer)
    return pl.pallas_call(
        kern,
        grid_spec=pltpu.PrefetchScalarGridSpec(
            num_scalar_prefetch=4,
            grid=(n_blocks,),
            in_specs=[
                pl.BlockSpec((EXPERT_ROWS * ch, LANES), rows),
                pl.BlockSpec(memory_space=pl.ANY),
                pl.BlockSpec(memory_space=pl.ANY),
                pl.BlockSpec(memory_space=pl.ANY),
            ],
            out_specs=pl.BlockSpec((EXPERT_ROWS * ch, LANES), out_rows),
            scratch_shapes=[
                pltpu.VMEM((2, D, F), F32), pltpu.VMEM((2, D, F), F32), pltpu.VMEM((2, F, D), F32),
                pltpu.VMEM((D, F), BF16), pltpu.VMEM((D, F), BF16), pltpu.VMEM((F, D), BF16),
                pltpu.SemaphoreType.DMA((2, 3)),
            ],
        ),
        out_shape=jax.ShapeDtypeStruct((n_blocks * EXPERT_ROWS * ch, LANES), U32),
        compiler_params=_cparams(1),
        name="experts",
    )(block_e, n_used, next_e, slot_e, xs, w_gate, w_up, w_down)


def _combine_kernel(dest_ref, wt_ref, hp_ref, x_ref, ada_ref, wsg_ref, wsu_ref, wsd_ref, fn_ref,
                    eo_ref, xo_ref, gbuf, sem, *, tc, final):
    half = x_ref.shape[1] // 2
    ch = half // LANES

    def issue(tp, carry):
        for u in range(ISSUE_UNROLL):
            t = tp * ISSUE_UNROLL + u
            t0 = pl.multiple_of(t * ch, ch)
            for k in range(MOE_TOP_K):
                r0 = pl.multiple_of(dest_ref[t * MOE_TOP_K + k] * ch, ch)
                pltpu.make_async_copy(eo_ref.at[pl.ds(r0, ch), :], gbuf.at[k, pl.ds(t0, ch), :],
                                      sem).start(priority=k % 2)
        return carry

    lax.fori_loop(0, tc // ISSUE_UNROLL, issue, 0)

    w = _load_rows(hp_ref, tc, ch)
    h = jnp.concatenate([_unpack_lo(w).astype(BF16), _unpack_hi(w).astype(BF16)], axis=1)
    gate = jnp.dot(h, wsg_ref[...], preferred_element_type=F32)
    up = jnp.dot(h, wsu_ref[...], preferred_element_type=F32)
    act = (_silu(gate) * up).astype(BF16)
    shared = jnp.dot(act, wsd_ref[...], preferred_element_type=F32)

    for k in range(MOE_TOP_K):
        pltpu.make_async_copy(eo_ref.at[pl.ds(0, tc * ch), :], gbuf.at[k], sem).wait()

    wts = wt_ref[...]
    ylo = shared[:, :half]
    yhi = shared[:, half:]
    for k in range(MOE_TOP_K):
        g = _load_rows(gbuf.at[k], tc, ch)
        wk = wts[:, k:k + 1]
        ylo = ylo + wk * _unpack_lo(g)
        yhi = yhi + wk * _unpack_hi(g)
    gf = ada_ref[5:6, :]
    xlo = x_ref[:, :half] + gf[:, :half] * ylo
    xhi = x_ref[:, half:] + gf[:, half:] * yhi
    if final:
        ss = (jnp.sum(xlo * xlo, axis=-1, keepdims=True)
              + jnp.sum(xhi * xhi, axis=-1, keepdims=True))
        inv = lax.rsqrt(ss / (2 * half) + NORM_EPS)
        xlo = xlo * inv * fn_ref[:, :half]
        xhi = xhi * inv * fn_ref[:, half:]
    xo_ref[:, :half] = xlo
    xo_ref[:, half:] = xhi


def _combine_call(dest_flat, wts, hp, x, ada_l, wsg, wsu, wsd, fnorm, eo, S, final):
    T, D = x.shape
    ch = D // 2 // LANES
    Fs = wsg.shape[1]
    tc = min(256, S)
    kern = functools.partial(_combine_kernel, tc=tc, final=final)
    return pl.pallas_call(
        kern,
        grid=(T // tc,),
        in_specs=[
            pl.BlockSpec((tc * MOE_TOP_K,), lambda i: (i,), memory_space=pltpu.SMEM),
            pl.BlockSpec((tc, MOE_TOP_K), lambda i: (i, 0)),
            pl.BlockSpec((tc * ch, LANES), lambda i: (i, 0)),
            pl.BlockSpec((tc, D), lambda i: (i, 0)),
            pl.BlockSpec((None, 6, D), lambda i: ((i * tc) // S, 0, 0)),
            pl.BlockSpec((D, Fs), lambda i: (0, 0)),
            pl.BlockSpec((D, Fs), lambda i: (0, 0)),
            pl.BlockSpec((Fs, D), lambda i: (0, 0)),
            pl.BlockSpec((1, D), lambda i: (0, 0)),
            pl.BlockSpec(memory_space=pl.ANY),
        ],
        out_specs=pl.BlockSpec((tc, D), lambda i: (i, 0)),
        out_shape=jax.ShapeDtypeStruct((T, D), F32),
        scratch_shapes=[pltpu.VMEM((MOE_TOP_K, tc * ch, LANES), U32), pltpu.SemaphoreType.DMA],
        compiler_params=_cparams(1),
        name="combine_shared",
    )(dest_flat, wts, hp, x, ada_l, wsg, wsu, wsd, fnorm.reshape(1, D), eo)


def _rope_tables(positions):
    half = ROPE_DIM // 2
    inv_freq = ROPE_THETA ** (-jnp.arange(0, ROPE_DIM, 2, dtype=F32) / ROPE_DIM)
    ang = positions.astype(F32).reshape(-1, 1) * inv_freq
    cos, sin = jnp.cos(ang), jnp.sin(ang)
    T = ang.shape[0]
    rest = HEAD_DIM - ROPE_DIM
    c = jnp.concatenate([cos, cos, jnp.ones((T, rest), F32)], axis=-1)
    s = jnp.concatenate([-sin, sin, jnp.zeros((T, rest), F32)], axis=-1)
    return c, s


def _expert_row_perm():
    r = jnp.arange(N_EXPERTS)
    return (r % N_GROUPS) * GROUP_SIZE + r // N_GROUPS


def kernel(x, c, positions, norm_mix, w_ada, b_ada, w_in, conv_w, w_branch, w_out, norm_ffn, w_router, router_bias, w_gate, w_up, w_down, ws_gate, ws_up, ws_down, final_norm):
    B, S, D = x.shape
    L = w_ada.shape[0]
    T = B * S
    A = w_branch.shape[2]
    H = A // HEAD_DIM
    assert S % MOBA_BLOCK == 0 and w_in.shape[2] == 6 * A + 2 * D

    n_assign = T * MOE_TOP_K
    n_blocks = -(-(n_assign + N_EXPERTS * (EXPERT_ROWS - 1)) // EXPERT_ROWS)

    rope_c, rope_s = _rope_tables(positions)
    ada = _ada_call(c, w_ada, b_ada).reshape(L, B, 6, D)
    perm = _expert_row_perm()
    xf = x.reshape(T, D)

    for l in range(L):
        proj = _inproj_call(xf, norm_mix[l], ada[l], w_in[l].astype(BF16), S)
        ya = _attn_call(proj, rope_c, rope_s, B, S, H)
        merged = _mix1_call(ya, proj, conv_w[l], w_branch[l, 0].astype(BF16),
                            w_branch[l, 1].astype(BF16), S, D)
        wr = w_router[l][:, perm]
        wr_hi = wr.astype(BF16)
        wr_lo = (wr - wr_hi.astype(F32)).astype(BF16)
        xf, hp, lgt = _mix2_call(merged, w_out[l].astype(BF16), xf, ada[l], norm_ffn[l],
                                 jnp.concatenate([wr_hi, wr_lo], axis=1), S)

        idx_t, wt_t, rs_t, cnt = _route_call(lgt, router_bias[l][perm], T)
        counts = cnt[perm, 0].astype(I32)
        padded = (counts + EXPERT_ROWS - 1) // EXPERT_ROWS * EXPERT_ROWS
        pad_ends = jnp.cumsum(padded)
        pad_starts = pad_ends - padded
        seg_start = jnp.sum(jnp.where(idx_t[:, :, None] == jnp.arange(N_EXPERTS), pad_starts, 0),
                            axis=-1)
        dest_flat = (seg_start + rs_t).T.reshape(-1)
        n_used = (pad_ends[-1] // EXPERT_ROWS).astype(I32)
        blk = jnp.minimum(jnp.arange(n_blocks, dtype=I32), n_used - 1)
        block_e = jnp.minimum(
            jnp.sum(pad_ends[None, :] <= (blk * EXPERT_ROWS)[:, None], axis=1),
            N_EXPERTS - 1).astype(I32)

        n_used = n_used.reshape(1)
        last_blk = jnp.where(padded > 0, pad_ends // EXPERT_ROWS - 1, -1).astype(I32)
        xs = _dispatch_call(hp, dest_flat, last_blk, n_used, n_blocks, D // 2 // LANES)
        present = padded > 0
        eids = jnp.arange(N_EXPERTS, dtype=I32)
        suffix_min = lax.cummin(jnp.where(present, eids, N_EXPERTS)[::-1])[::-1]
        next_e = jnp.concatenate([suffix_min[1:], jnp.full((1,), N_EXPERTS, I32)])
        next_e = jnp.where(next_e >= N_EXPERTS, -1, next_e).astype(I32)
        slot_e = ((jnp.cumsum(present) - present) % 2).astype(I32)
        eo = _expert_call(xs, block_e, n_used, next_e, slot_e, w_gate, w_up, w_down, l, n_blocks)
        xf = _combine_call(dest_flat, wt_t.T, hp, xf, ada[l], ws_gate[l].astype(BF16),
                           ws_up[l].astype(BF16), ws_down[l].astype(BF16), final_norm, eo, S,
                           final=(l == L - 1))
    return xf.reshape(B, S, D)
```

```python
import functools

import jax
import jax.numpy as jnp
from jax import lax
from jax.experimental import pallas as pl
from jax.experimental.pallas import tpu as pltpu

F32 = jnp.float32
BF16 = jnp.bfloat16
U32 = jnp.uint32
I32 = jnp.int32

HEAD_DIM = 128
ROPE_DIM = HEAD_DIM // 4
ROPE_THETA = 500000.0
MOBA_BLOCK = 256
MOBA_TOPK = 3
CONV_LEN = 3
N_EXPERTS = 64
MOE_TOP_K = 8
N_GROUPS = 8
GROUP_SIZE = N_EXPERTS // N_GROUPS
TOPK_GROUPS = 4
ROUTED_SCALE = 2.5
NORM_EPS = 1e-6

VMEM_LIMIT_BYTES = 56 * 1024 * 1024
SUBLANES = 8
EXPERT_ROWS = 256
EXPERT_SUBBLOCKS = 1
ATTN_HEAD_GROUP = 8
ISSUE_UNROLL = 4
NEG_BIG = -1e30
LOG2_E = 1.4426950408889634
SUM_ROWS = 16

NT_DIMS = (((1,), (1,)), ((), ()))


def _cparams(n_axes):
    return pltpu.CompilerParams(
        dimension_semantics=("arbitrary",) * n_axes,
        vmem_limit_bytes=VMEM_LIMIT_BYTES)


def _bits(v):
    return lax.bitcast_convert_type(v, U32)


def _pack_halves(v):
    n = v.shape[1] // 2
    lo = _bits(v[:, :n].astype(BF16).astype(F32))
    hi = _bits(v[:, n:].astype(BF16).astype(F32))
    return hi | (lo >> 16)


def _unpack_lo(w):
    return lax.bitcast_convert_type(w << 16, F32)


def _unpack_hi(w):
    return lax.bitcast_convert_type(w & jnp.uint32(0xFFFF0000), F32)


LANES = 128


def _load_rows(ref, n, ch):
    return jnp.concatenate([ref[pl.ds(s, n, stride=ch), :] for s in range(ch)], axis=1)


def _store_rows(ref, val, n, ch):
    for s in range(ch):
        ref[pl.ds(s, n, stride=ch), :] = val[:, s * LANES:(s + 1) * LANES]


def _silu(v):
    return v * jax.nn.sigmoid(v)


def _rms_modulate(x, g, shift, scale):
    y = x * lax.rsqrt(jnp.mean(x * x, axis=-1, keepdims=True) + NORM_EPS)
    return (y * g) * (1.0 + scale) + shift


def _ada_kernel(c_ref, w_ref, b_ref, o_ref):
    ca = _silu(c_ref[...]).astype(BF16)
    o_ref[0] = jnp.dot(ca, w_ref[0].astype(BF16), preferred_element_type=F32) + b_ref[0]


def _ada_call(c, w_ada, b_ada):
    L, D, N = w_ada.shape
    B = c.shape[0]
    tn = min(512, N)
    return pl.pallas_call(
        _ada_kernel,
        grid=(L, N // tn),
        in_specs=[
            pl.BlockSpec((B, D), lambda l, j: (0, 0)),
            pl.BlockSpec((1, D, tn), lambda l, j: (l, 0, j)),
            pl.BlockSpec((1, 1, tn), lambda l, j: (l, 0, j)),
        ],
        out_specs=pl.BlockSpec((1, B, tn), lambda l, j: (l, 0, j)),
        out_shape=jax.ShapeDtypeStruct((L, B, N), F32),
        compiler_params=_cparams(2),
        name="ada",
    )(c, w_ada, b_ada.reshape(L, 1, N))


def _inproj_kernel(x_ref, g_ref, ada_ref, w_ref, o_ref, h_sc):
    @pl.when(pl.program_id(1) == 0)
    def _():
        h = _rms_modulate(x_ref[...], g_ref[...], ada_ref[0:1, :], ada_ref[1:2, :])
        h_sc[...] = h.astype(BF16)

    o_ref[...] = jnp.dot(h_sc[...], w_ref[...], preferred_element_type=F32).astype(o_ref.dtype)


def _inproj_call(x, g, ada_l, w, S):
    T, D = x.shape
    N = w.shape[1]
    tm = min(1024, S)
    tn = min(1024, N)
    return pl.pallas_call(
        _inproj_kernel,
        grid=(T // tm, N // tn),
        in_specs=[
            pl.BlockSpec((tm, D), lambda i, j: (i, 0)),
            pl.BlockSpec((1, D), lambda i, j: (0, 0)),
            pl.BlockSpec((None, 6, D), lambda i, j: ((i * tm) // S, 0, 0)),
            pl.BlockSpec((D, tn), lambda i, j: (0, j)),
        ],
        out_specs=pl.BlockSpec((tm, tn), lambda i, j: (i, j)),
        out_shape=jax.ShapeDtypeStruct((T, N), BF16),
        scratch_shapes=[pltpu.VMEM((tm, D), BF16)],
        compiler_params=_cparams(2),
        name="inproj",
    )(x, g.reshape(1, D), ada_l, w)


def _rope(v, c, s):
    half = ROPE_DIM // 2
    lane = lax.broadcasted_iota(I32, v.shape, 1)
    partner = jnp.where(lane < half,
                        pltpu.roll(v, HEAD_DIM - half, axis=1),
                        pltpu.roll(v, half, axis=1))
    return v * c + partner * s


def _attn_kernel(q_ref, k_ref, v_ref, cq_ref, sq_ref, ck_ref, sk_ref, o_ref,
                 kr_sc, vt_sc, km_sc, sel_sc, qs_sc, s_sc, s2_sc, m_sc, acc_sc, *, nb, hg):
    qi = pl.program_id(2)
    blk = MOBA_BLOCK
    dh = HEAD_DIM
    cols = lambda h: slice(h * dh, (h + 1) * dh)

    @pl.when(qi == 0)
    def _():
        km_sc[...] = jnp.zeros_like(km_sc)
        for j in range(nb):
            rows = pl.ds(j * blk, blk)
            ck = ck_ref[rows, :]
            sk = sk_ref[rows, :]
            for h in range(hg):
                kr = _rope(k_ref[rows, cols(h)].astype(F32), ck, sk)
                kr_sc[h, j] = kr.astype(BF16)
                km_sc[h, j:j + 1, :] = jnp.mean(kr, axis=0, keepdims=True)
                vt = v_ref[rows, cols(h)].astype(F32).T.astype(BF16)
                vt_sc[h, j] = jnp.concatenate([vt, jnp.ones((SUM_ROWS, blk), BF16)], axis=0)

    cq = cq_ref[...]
    sq = sq_ref[...]
    kidx = lax.broadcasted_iota(I32, (blk, blk), 0)
    qidx = lax.broadcasted_iota(I32, (blk, blk), 1)
    causal = kidx <= qidx
    for h in range(hg):
        q = _rope(q_ref[:, cols(h)].astype(F32), cq, sq)
        sel_sc[h] = lax.dot_general(km_sc[h], q, NT_DIMS, precision=lax.Precision.HIGHEST,
                                    preferred_element_type=F32)
        qs = (q * (HEAD_DIM ** -0.5 * LOG2_E)).astype(BF16)
        qs_sc[h] = qs
        st = lax.dot_general(kr_sc[h, qi], qs, NT_DIMS, preferred_element_type=F32)
        s_sc[h] = jnp.where(causal, st, NEG_BIG)

    for h in range(hg):
        gt = sel_sc[h]
        row = lax.broadcasted_iota(I32, gt.shape, 0)
        past = row < qi
        gm = jnp.where(past, gt, -jnp.inf)
        rank = jnp.zeros(gt.shape, F32)
        for jp in range(nb):
            gj = gm[jp:jp + 1, :]
            beats = (gj > gm) | ((gj == gm) & (jp < row))
            rank = rank + beats.astype(F32)
        sel_sc[h] = (past & (rank < MOBA_TOPK)).astype(F32)

    for h in range(hg):
        st = s_sc[h]
        m0 = jnp.max(st, axis=0, keepdims=True)
        p0 = jnp.exp2(st - m0)
        m_sc[h] = m0
        acc_sc[h] = jnp.dot(vt_sc[h, qi], p0.astype(BF16), preferred_element_type=F32)

    def stage_scores(j, buf):
        jc = jnp.minimum(j, nb - 1)
        for h in range(hg):
            buf[h] = lax.dot_general(kr_sc[h, jc], qs_sc[h], NT_DIMS, preferred_element_type=F32)

    def consume(j, buf):
        jc = jnp.minimum(j, nb - 1)
        for h in range(hg):
            sj = jnp.where(sel_sc[h, pl.ds(jc, 1), :] > 0.0, buf[h], NEG_BIG)
            m_old = m_sc[h]
            m_new = jnp.maximum(m_old, jnp.max(sj, axis=0, keepdims=True))
            alpha = jnp.exp2(m_old - m_new)
            p = jnp.exp2(sj - m_new)
            acc_sc[h] = alpha * acc_sc[h] + jnp.dot(vt_sc[h, jc], p.astype(BF16),
                                                    preferred_element_type=F32)
            m_sc[h] = m_new

    stage_scores(0, s_sc)

    def past_pair(jj, carry):
        a = 2 * jj
        stage_scores(a + 1, s2_sc)
        consume(a, s_sc)
        stage_scores(a + 2, s_sc)
        consume(a + 1, s2_sc)
        return carry

    lax.fori_loop(0, (qi + 1) // 2, past_pair, 0)

    for h in range(hg):
        o = acc_sc[h, :dh, :] / acc_sc[h, dh:dh + 1, :]
        o_ref[:, cols(h)] = o.T.astype(o_ref.dtype)


def _attn_call(proj, rope_c, rope_s, B, S, H):
    T = B * S
    blk = MOBA_BLOCK
    nb = S // blk
    nbp = -(-nb // SUBLANES) * SUBLANES
    dh = HEAD_DIM
    hg = min(ATTN_HEAD_GROUP, H)
    ng = H // hg
    w = hg * dh
    kern = functools.partial(_attn_kernel, nb=nb, hg=hg)
    return pl.pallas_call(
        kern,
        grid=(B, ng, nb),
        in_specs=[
            pl.BlockSpec((blk, w), lambda b, g, i: (b * nb + i, g)),
            pl.BlockSpec((S, w), lambda b, g, i: (b, ng + g)),
            pl.BlockSpec((S, w), lambda b, g, i: (b, 2 * ng + g)),
            pl.BlockSpec((blk, dh), lambda b, g, i: (b * nb + i, 0)),
            pl.BlockSpec((blk, dh), lambda b, g, i: (b * nb + i, 0)),
            pl.BlockSpec((S, dh), lambda b, g, i: (b, 0)),
            pl.BlockSpec((S, dh), lambda b, g, i: (b, 0)),
        ],
        out_specs=pl.BlockSpec((blk, w), lambda b, g, i: (b * nb + i, g)),
        out_shape=jax.ShapeDtypeStruct((T, H * dh), BF16),
        scratch_shapes=[
            pltpu.VMEM((hg, nb, blk, dh), BF16),
            pltpu.VMEM((hg, nb, dh + SUM_ROWS, blk), BF16),
            pltpu.VMEM((hg, nbp, dh), F32),
            pltpu.VMEM((hg, nbp, blk), F32),
            pltpu.VMEM((hg, blk, dh), BF16),
            pltpu.VMEM((hg, blk, blk), F32),
            pltpu.VMEM((hg, blk, blk), F32),
            pltpu.VMEM((hg, 1, blk), F32),
            pltpu.VMEM((hg, dh + SUM_ROWS, blk), F32),
        ],
        compiler_params=_cparams(3),
        name="moba_attn",
    )(proj, proj, proj, rope_c, rope_s, rope_c, rope_s)


def _mix1_kernel(ya_ref, cb_ref, cc_ref, cx_ref, hcc_ref, hcx_ref, ga_ref, gc_ref, cw_ref,
                 wb0_ref, wb1_ref, o_ref, *, tm, S):
    i = pl.program_id(0)
    u = cc_ref[...].astype(F32) * cx_ref[...].astype(F32)
    hu = hcc_ref[...].astype(F32) * hcx_ref[...].astype(F32)
    seq_start = (i * tm) % S == 0
    hu = jnp.where(seq_start, 0.0, hu)
    h1 = hu[SUBLANES - 1:SUBLANES, :]
    h2 = hu[SUBLANES - 2:SUBLANES - 1, :]
    r = lax.broadcasted_iota(I32, u.shape, 0)
    u1 = jnp.where(r == 0, h1, pltpu.roll(u, 1, axis=0))
    u2 = jnp.where(r == 0, h2, jnp.where(r == 1, h1, pltpu.roll(u, 2, axis=0)))
    conv = cw_ref[0:1, :] * u2 + cw_ref[1:2, :] * u1 + cw_ref[2:3, :] * u
    yc = (cb_ref[...].astype(F32) * conv).astype(BF16)
    a = jnp.dot(ya_ref[...], wb0_ref[...], preferred_element_type=F32)
    c = jnp.dot(yc, wb1_ref[...], preferred_element_type=F32)
    merged = (jax.nn.sigmoid(ga_ref[...].astype(F32)) * a
              + jax.nn.sigmoid(gc_ref[...].astype(F32)) * c)
    o_ref[...] = merged.astype(o_ref.dtype)


def _mix1_call(ya, proj, conv_w, wb0, wb1, S, D):
    T, A = ya.shape
    assert conv_w.shape == (CONV_LEN, A) and (6 * A) % D == 0
    tm = min(256, S)
    gcol = 6 * A // D
    hb = tm // SUBLANES
    kern = functools.partial(_mix1_kernel, tm=tm, S=S)
    halo = lambda col: pl.BlockSpec((SUBLANES, A), lambda i: (jnp.maximum(i * hb - 1, 0), col))
    return pl.pallas_call(
        kern,
        grid=(T // tm,),
        in_specs=[
            pl.BlockSpec((tm, A), lambda i: (i, 0)),
            pl.BlockSpec((tm, A), lambda i: (i, 3)),
            pl.BlockSpec((tm, A), lambda i: (i, 4)),
            pl.BlockSpec((tm, A), lambda i: (i, 5)),
            halo(4),
            halo(5),
            pl.BlockSpec((tm, D), lambda i: (i, gcol)),
            pl.BlockSpec((tm, D), lambda i: (i, gcol + 1)),
            pl.BlockSpec((CONV_LEN, A), lambda i: (0, 0)),
            pl.BlockSpec((A, D), lambda i: (0, 0)),
            pl.BlockSpec((A, D), lambda i: (0, 0)),
        ],
        out_specs=pl.BlockSpec((tm, D), lambda i: (i, 0)),
        out_shape=jax.ShapeDtypeStruct((T, D), BF16),
        compiler_params=_cparams(1),
        name="mix_merge",
    )(ya, proj, proj, proj, proj, proj, proj, proj, conv_w, wb0, wb1)


def _mix2_kernel(m_ref, wo_ref, x_ref, ada_ref, g_ref, wr_ref, xo_ref, hp_ref, lg_ref):
    out = jnp.dot(m_ref[...], wo_ref[...], preferred_element_type=F32)
    x = x_ref[...] + ada_ref[2:3, :] * out
    xo_ref[...] = x
    h = _rms_modulate(x, g_ref[...], ada_ref[3:4, :], ada_ref[4:5, :])
    tm, d = x.shape
    _store_rows(hp_ref, _pack_halves(h), tm, d // 2 // LANES)
    e = wr_ref.shape[1] // 2
    h_hi = h.astype(BF16)
    h_lo = (h - h_hi.astype(F32)).astype(BF16)
    lg2 = jnp.dot(h_hi, wr_ref[...], preferred_element_type=F32)
    lg1 = jnp.dot(h_lo, wr_ref[...], preferred_element_type=F32)
    lg = lg2[:, :e] + lg2[:, e:] + lg1[:, :e]
    lg_ref[...] = lg.T


def _mix2_call(merged, w_out, x, ada_l, g, wr, S):
    T, D = x.shape
    tm = min(256, S)
    E = wr.shape[1] // 2
    ch = D // 2 // LANES
    return pl.pallas_call(
        _mix2_kernel,
        grid=(T // tm,),
        in_specs=[
            pl.BlockSpec((tm, D), lambda i: (i, 0)),
            pl.BlockSpec((D, D), lambda i: (0, 0)),
            pl.BlockSpec((tm, D), lambda i: (i, 0)),
            pl.BlockSpec((None, 6, D), lambda i: ((i * tm) // S, 0, 0)),
            pl.BlockSpec((1, D), lambda i: (0, 0)),
            pl.BlockSpec((D, 2 * E), lambda i: (0, 0)),
        ],
        out_specs=[
            pl.BlockSpec((tm, D), lambda i: (i, 0)),
            pl.BlockSpec((tm * ch, LANES), lambda i: (i, 0)),
            pl.BlockSpec((E, tm), lambda i: (0, i)),
        ],
        out_shape=[
            jax.ShapeDtypeStruct((T, D), F32),
            jax.ShapeDtypeStruct((T * ch, LANES), U32),
            jax.ShapeDtypeStruct((E, T), F32),
        ],
        compiler_params=_cparams(1),
        name="outproj_norm_router",
    )(merged, w_out, x, ada_l, g.reshape(1, D), wr)


def _route_kernel(lg_ref, bias_ref, tri_ref, idx_ref, wt_ref, rs_ref, cnt_ref, carry_sc):
    step = pl.program_id(0)

    @pl.when(step == 0)
    def _():
        carry_sc[...] = jnp.zeros_like(carry_sc)

    G = N_GROUPS
    scores = jax.nn.sigmoid(lg_ref[...])
    biased = scores + bias_ref[...]
    sc = [scores[i * G:(i + 1) * G, :] for i in range(GROUP_SIZE)]
    bi = [biased[i * G:(i + 1) * G, :] for i in range(GROUP_SIZE)]
    grow = lax.broadcasted_iota(I32, bi[0].shape, 0)
    eid = [grow * GROUP_SIZE + i for i in range(GROUP_SIZE)]
    neg_inf = jnp.float32(-jnp.inf)

    m1 = functools.reduce(jnp.maximum, bi)
    first = jnp.full(grow.shape, GROUP_SIZE, I32)
    for i in reversed(range(GROUP_SIZE)):
        first = jnp.where(bi[i] == m1, i, first)
    m2 = functools.reduce(jnp.maximum,
                          [jnp.where(first == i, neg_inf, bi[i]) for i in range(GROUP_SIZE)])
    gs = m1 + m2

    grank = jnp.zeros(gs.shape, F32)
    for gp in range(G):
        v = gs[gp:gp + 1, :]
        grank = grank + ((v > gs) | ((v == gs) & (gp < grow))).astype(F32)
    gsel = grank < TOPK_GROUPS

    ms = [jnp.where(gsel, b, neg_inf) for b in bi]
    chosen = [jnp.zeros(grow.shape, jnp.bool_) for _ in range(GROUP_SIZE)]
    picked_idx = []
    picked_score = []
    for _ in range(MOE_TOP_K):
        mx = jnp.max(functools.reduce(jnp.maximum, ms), axis=0, keepdims=True)
        cand = functools.reduce(
            jnp.minimum, [jnp.where(ms[i] == mx, eid[i], N_EXPERTS) for i in range(GROUP_SIZE)])
        ce = jnp.min(cand, axis=0, keepdims=True)
        hit = [eid[i] == ce for i in range(GROUP_SIZE)]
        s_k = jnp.sum(functools.reduce(
            jnp.add, [jnp.where(hit[i], sc[i], 0.0) for i in range(GROUP_SIZE)]),
            axis=0, keepdims=True)
        ms = [jnp.where(hit[i], neg_inf, ms[i]) for i in range(GROUP_SIZE)]
        chosen = [chosen[i] | hit[i] for i in range(GROUP_SIZE)]
        picked_idx.append(ce)
        picked_score.append(s_k)

    wsum = functools.reduce(jnp.add, picked_score)
    for k in range(MOE_TOP_K):
        idx_ref[k:k + 1, :] = picked_idx[k]
        wt_ref[k:k + 1, :] = picked_score[k] / wsum * ROUTED_SCALE

    selm = jnp.concatenate([c.astype(F32) for c in chosen], axis=0)
    pre = jnp.dot(selm.astype(BF16), tri_ref[...], preferred_element_type=F32) + carry_sc[:, 0:1]
    pre_slab = [pre[i * G:(i + 1) * G, :] for i in range(GROUP_SIZE)]
    for k in range(MOE_TOP_K):
        rk = jnp.sum(functools.reduce(
            jnp.add, [jnp.where(eid[i] == picked_idx[k], pre_slab[i], 0.0)
                      for i in range(GROUP_SIZE)]), axis=0, keepdims=True)
        rs_ref[k:k + 1, :] = rk.astype(I32)
    carry_sc[...] = carry_sc[...] + jnp.sum(selm, axis=1, keepdims=True)
    cnt_ref[...] = carry_sc[...]


def _route_call(lgt, bias_perm, T):
    E = lgt.shape[0]
    tr = min(512, T)
    tri = (lax.broadcasted_iota(I32, (tr, tr), 0) < lax.broadcasted_iota(I32, (tr, tr), 1)).astype(BF16)
    return pl.pallas_call(
        _route_kernel,
        grid=(T // tr,),
        in_specs=[
            pl.BlockSpec((E, tr), lambda i: (0, i)),
            pl.BlockSpec((E, 1), lambda i: (0, 0)),
            pl.BlockSpec((tr, tr), lambda i: (0, 0)),
        ],
        out_specs=[
            pl.BlockSpec((MOE_TOP_K, tr), lambda i: (0, i)),
            pl.BlockSpec((MOE_TOP_K, tr), lambda i: (0, i)),
            pl.BlockSpec((MOE_TOP_K, tr), lambda i: (0, i)),
            pl.BlockSpec((E, 128), lambda i: (0, 0)),
        ],
        out_shape=[
            jax.ShapeDtypeStruct((MOE_TOP_K, T), I32),
            jax.ShapeDtypeStruct((MOE_TOP_K, T), F32),
            jax.ShapeDtypeStruct((MOE_TOP_K, T), I32),
            jax.ShapeDtypeStruct((E, 128), F32),
        ],
        scratch_shapes=[pltpu.VMEM((E, 128), F32)],
        compiler_params=_cparams(1),
        name="route",
    )(lgt, bias_perm.reshape(E, 1), tri)


def _dispatch_kernel(zs_ref, nu_ref, hp_ref, dest_ref, xs_ref, z_sc, sem, *, td, ch, n_blocks):
    i = pl.program_id(0)
    zrows = EXPERT_ROWS * ch

    @pl.when(i == 0)
    def _():
        z_sc[...] = jnp.zeros_like(z_sc)
        zero_wait = pltpu.make_async_copy(z_sc, xs_ref.at[pl.ds(0, zrows), :], sem)

        def zero_block(b, carry):
            b0 = pl.multiple_of(b * zrows, zrows)
            pltpu.make_async_copy(z_sc, xs_ref.at[pl.ds(b0, zrows), :], sem).start()
            return carry

        def wait_block(b, carry):
            zero_wait.wait()
            return carry

        for e in range(N_EXPERTS):
            @pl.when(zs_ref[e] >= 0)
            def _(e=e):
                zero_block(zs_ref[e], 0)
        lax.fori_loop(nu_ref[0], n_blocks, zero_block, 0)
        for e in range(N_EXPERTS):
            @pl.when(zs_ref[e] >= 0)
            def _():
                zero_wait.wait()
        lax.fori_loop(nu_ref[0], n_blocks, wait_block, 0)

    def issue(tp, carry):
        for u in range(ISSUE_UNROLL):
            t = tp * ISSUE_UNROLL + u
            src = hp_ref.at[pl.ds(pl.multiple_of(t * ch, ch), ch), :]
            for k in range(MOE_TOP_K):
                r0 = pl.multiple_of(dest_ref[t * MOE_TOP_K + k] * ch, ch)
                pltpu.make_async_copy(src, xs_ref.at[pl.ds(r0, ch), :], sem).start(priority=k % 2)
        return carry

    lax.fori_loop(0, td // ISSUE_UNROLL, issue, 0)
    for k in range(MOE_TOP_K):
        pltpu.make_async_copy(hp_ref, xs_ref.at[pl.ds(0, td * ch), :], sem).wait()


def _dispatch_call(hp, dest_flat, zstart, n_used, n_blocks, ch):
    T = hp.shape[0] // ch
    td = min(256, T)
    n_rows = n_blocks * EXPERT_ROWS
    kern = functools.partial(_dispatch_kernel, td=td, ch=ch, n_blocks=n_blocks)
    return pl.pallas_call(
        kern,
        grid_spec=pltpu.PrefetchScalarGridSpec(
            num_scalar_prefetch=2,
            grid=(T // td,),
            in_specs=[
                pl.BlockSpec((td * ch, LANES), lambda i, zs, nu: (i, 0)),
                pl.BlockSpec((td * MOE_TOP_K,), lambda i, zs, nu: (i,), memory_space=pltpu.SMEM),
            ],
            out_specs=pl.BlockSpec(memory_space=pl.ANY),
            scratch_shapes=[pltpu.VMEM((EXPERT_ROWS * ch, LANES), U32), pltpu.SemaphoreType.DMA],
        ),
        out_shape=jax.ShapeDtypeStruct((n_rows * ch, LANES), U32),
        compiler_params=_cparams(1),
        name="dispatch",
    )(zstart, n_used, hp, dest_flat)


def _expert_kernel(be_ref, nu_ref, nx_ref, sl_ref, xs_ref, wg_hbm, wu_hbm, wd_hbm, o_ref,
                   wg_st, wu_st, wd_st, wg_sc, wu_sc, wd_sc, sems, *, layer):
    i = pl.program_id(0)
    e = be_ref[i]
    prev = be_ref[jnp.maximum(i - 1, 0)]
    changed = (i == 0) | (e != prev)

    def weight_copies(ex, slot):
        return (pltpu.make_async_copy(wg_hbm.at[layer, ex], wg_st.at[slot], sems.at[slot, 0]),
                pltpu.make_async_copy(wu_hbm.at[layer, ex], wu_st.at[slot], sems.at[slot, 1]),
                pltpu.make_async_copy(wd_hbm.at[layer, ex], wd_st.at[slot], sems.at[slot, 2]))

    @pl.when(i == 0)
    def _():
        for cp in weight_copies(e, sl_ref[e]):
            cp.start(priority=1)

    @pl.when(changed)
    def _():
        slot = sl_ref[e]
        for cp in weight_copies(e, slot):
            cp.wait()
        wg_sc[...] = wg_st[slot].astype(BF16)
        wu_sc[...] = wu_st[slot].astype(BF16)
        wd_sc[...] = wd_st[slot].astype(BF16)
        nxt = nx_ref[e]

        @pl.when(nxt >= 0)
        def _():
            for cp in weight_copies(nxt, 1 - slot):
                cp.start(priority=1)

    @pl.when(i < nu_ref[0])
    def _():
        ch = wg_sc.shape[0] // 2 // LANES
        sub = EXPERT_ROWS // EXPERT_SUBBLOCKS
        gu = []
        for s in range(EXPERT_SUBBLOCKS):
            w = _load_rows(xs_ref.at[pl.ds(s * sub * ch, sub * ch), :], sub, ch)
            x = jnp.concatenate([_unpack_lo(w).astype(BF16), _unpack_hi(w).astype(BF16)], axis=1)
            gu.append((jnp.dot(x, wg_sc[...], preferred_element_type=F32),
                       jnp.dot(x, wu_sc[...], preferred_element_type=F32)))
        for s in range(EXPERT_SUBBLOCKS):
            gate, up = gu[s]
            act = (_silu(gate) * up).astype(BF16)
            out = jnp.dot(act, wd_sc[...], preferred_element_type=F32)
            _store_rows(o_ref.at[pl.ds(s * sub * ch, sub * ch), :], _pack_halves(out), sub, ch)

    @pl.when(i >= nu_ref[0])
    def _():
        o_ref[...] = jnp.zeros_like(o_ref)


def _expert_call(xs, block_e, n_used, next_e, slot_e, w_gate, w_up, w_down, layer, n_blocks):
    _, E, D, F = w_gate.shape
    ch = D // 2 // LANES
    rows = lambda i, be, nu, nx, sl: (jnp.minimum(i, nu[0] - 1), 0)
    out_rows = lambda i, be, nu, nx, sl: (i, 0)
    kern = functools.partial(_expert_kernel, layer=layer)
    return pl.pallas_call(
        kern,
        grid_spec=pltpu.PrefetchScalarGridSpec(
            num_scalar_prefetch=4,
            grid=(n_blocks,),
            in_specs=[
                pl.BlockSpec((EXPERT_ROWS * ch, LANES), rows),
                pl.BlockSpec(memory_space=pl.ANY),
                pl.BlockSpec(memory_space=pl.ANY),
                pl.BlockSpec(memory_space=pl.ANY),
            ],
            out_specs=pl.BlockSpec((EXPERT_ROWS * ch, LANES), out_rows),
            scratch_shapes=[
                pltpu.VMEM((2, D, F), F32), pltpu.VMEM((2, D, F), F32), pltpu.VMEM((2, F, D), F32),
                pltpu.VMEM((D, F), BF16), pltpu.VMEM((D, F), BF16), pltpu.VMEM((F, D), BF16),
                pltpu.SemaphoreType.DMA((2, 3)),
            ],
        ),
        out_shape=jax.ShapeDtypeStruct((n_blocks * EXPERT_ROWS * ch, LANES), U32),
        compiler_params=_cparams(1),
        name="experts",
    )(block_e, n_used, next_e, slot_e, xs, w_gate, w_up, w_down)


def _combine_kernel(dest_ref, wt_ref, hp_ref, x_ref, ada_ref, wsg_ref, wsu_ref, wsd_ref, fn_ref,
                    eo_ref, xo_ref, gbuf, sem, *, tc, final):
    half = x_ref.shape[1] // 2
    ch = half // LANES

    def issue(tp, carry):
        for u in range(ISSUE_UNROLL):
            t = tp * ISSUE_UNROLL + u
            t0 = pl.multiple_of(t * ch, ch)
            for k in range(MOE_TOP_K):
                r0 = pl.multiple_of(dest_ref[t * MOE_TOP_K + k] * ch, ch)
                pltpu.make_async_copy(eo_ref.at[pl.ds(r0, ch), :], gbuf.at[k, pl.ds(t0, ch), :],
                                      sem).start(priority=k % 2)
        return carry

    lax.fori_loop(0, tc // ISSUE_UNROLL, issue, 0)

    w = _load_rows(hp_ref, tc, ch)
    h = jnp.concatenate([_unpack_lo(w).astype(BF16), _unpack_hi(w).astype(BF16)], axis=1)
    gate = jnp.dot(h, wsg_ref[...], preferred_element_type=F32)
    up = jnp.dot(h, wsu_ref[...], preferred_element_type=F32)
    act = (_silu(gate) * up).astype(BF16)
    shared = jnp.dot(act, wsd_ref[...], preferred_element_type=F32)

    for k in range(MOE_TOP_K):
        pltpu.make_async_copy(eo_ref.at[pl.ds(0, tc * ch), :], gbuf.at[k], sem).wait()

    wts = wt_ref[...]
    ylo = shared[:, :half]
    yhi = shared[:, half:]
    for k in range(MOE_TOP_K):
        g = _load_rows(gbuf.at[k], tc, ch)
        wk = wts[:, k:k + 1]
        ylo = ylo + wk * _unpack_lo(g)
        yhi = yhi + wk * _unpack_hi(g)
    gf = ada_ref[5:6, :]
    xlo = x_ref[:, :half] + gf[:, :half] * ylo
    xhi = x_ref[:, half:] + gf[:, half:] * yhi
    if final:
        ss = (jnp.sum(xlo * xlo, axis=-1, keepdims=True)
              + jnp.sum(xhi * xhi, axis=-1, keepdims=True))
        inv = lax.rsqrt(ss / (2 * half) + NORM_EPS)
        xlo = xlo * inv * fn_ref[:, :half]
        xhi = xhi * inv * fn_ref[:, half:]
    xo_ref[:, :half] = xlo
    xo_ref[:, half:] = xhi


def _combine_call(dest_flat, wts, hp, x, ada_l, wsg, wsu, wsd, fnorm, eo, S, final):
    T, D = x.shape
    ch = D // 2 // LANES
    Fs = wsg.shape[1]
    tc = min(256, S)
    kern = functools.partial(_combine_kernel, tc=tc, final=final)
    return pl.pallas_call(
        kern,
        grid=(T // tc,),
        in_specs=[
            pl.BlockSpec((tc * MOE_TOP_K,), lambda i: (i,), memory_space=pltpu.SMEM),
            pl.BlockSpec((tc, MOE_TOP_K), lambda i: (i, 0)),
            pl.BlockSpec((tc * ch, LANES), lambda i: (i, 0)),
            pl.BlockSpec((tc, D), lambda i: (i, 0)),
            pl.BlockSpec((None, 6, D), lambda i: ((i * tc) // S, 0, 0)),
            pl.BlockSpec((D, Fs), lambda i: (0, 0)),
            pl.BlockSpec((D, Fs), lambda i: (0, 0)),
            pl.BlockSpec((Fs, D), lambda i: (0, 0)),
            pl.BlockSpec((1, D), lambda i: (0, 0)),
            pl.BlockSpec(memory_space=pl.ANY),
        ],
        out_specs=pl.BlockSpec((tc, D), lambda i: (i, 0)),
        out_shape=jax.ShapeDtypeStruct((T, D), F32),
        scratch_shapes=[pltpu.VMEM((MOE_TOP_K, tc * ch, LANES), U32), pltpu.SemaphoreType.DMA],
        compiler_params=_cparams(1),
        name="combine_shared",
    )(dest_flat, wts, hp, x, ada_l, wsg, wsu, wsd, fnorm.reshape(1, D), eo)


def _rope_tables(positions):
    half = ROPE_DIM // 2
    inv_freq = ROPE_THETA ** (-jnp.arange(0, ROPE_DIM, 2, dtype=F32) / ROPE_DIM)
    ang = positions.astype(F32).reshape(-1, 1) * inv_freq
    cos, sin = jnp.cos(ang), jnp.sin(ang)
    T = ang.shape[0]
    rest = HEAD_DIM - ROPE_DIM
    c = jnp.concatenate([cos, cos, jnp.ones((T, rest), F32)], axis=-1)
    s = jnp.concatenate([-sin, sin, jnp.zeros((T, rest), F32)], axis=-1)
    return c, s


def _expert_row_perm():
    r = jnp.arange(N_EXPERTS)
    return (r % N_GROUPS) * GROUP_SIZE + r // N_GROUPS


def kernel(x, c, positions, norm_mix, w_ada, b_ada, w_in, conv_w, w_branch, w_out, norm_ffn, w_router, router_bias, w_gate, w_up, w_down, ws_gate, ws_up, ws_down, final_norm):
    B, S, D = x.shape
    L = w_ada.shape[0]
    T = B * S
    A = w_branch.shape[2]
    H = A // HEAD_DIM
    assert S % MOBA_BLOCK == 0 and w_in.shape[2] == 6 * A + 2 * D

    n_assign = T * MOE_TOP_K
    n_blocks = -(-(n_assign + N_EXPERTS * (EXPERT_ROWS - 1)) // EXPERT_ROWS)

    rope_c, rope_s = _rope_tables(positions)
    ada = _ada_call(c, w_ada, b_ada).reshape(L, B, 6, D)
    perm = _expert_row_perm()
    xf = x.reshape(T, D)

    for l in range(L):
        proj = _inproj_call(xf, norm_mix[l], ada[l], w_in[l].astype(BF16), S)
        ya = _attn_call(proj, rope_c, rope_s, B, S, H)
        merged = _mix1_call(ya, proj, conv_w[l], w_branch[l, 0].astype(BF16),
                            w_branch[l, 1].astype(BF16), S, D)
        wr = w_router[l][:, perm]
        wr_hi = wr.astype(BF16)
        wr_lo = (wr - wr_hi.astype(F32)).astype(BF16)
        xf, hp, lgt = _mix2_call(merged, w_out[l].astype(BF16), xf, ada[l], norm_ffn[l],
                                 jnp.concatenate([wr_hi, wr_lo], axis=1), S)

        idx_t, wt_t, rs_t, cnt = _route_call(lgt, router_bias[l][perm], T)
        counts = cnt[perm, 0].astype(I32)
        padded = (counts + EXPERT_ROWS - 1) // EXPERT_ROWS * EXPERT_ROWS
        pad_ends = jnp.cumsum(padded)
        pad_starts = pad_ends - padded
        seg_start = jnp.sum(jnp.where(idx_t[:, :, None] == jnp.arange(N_EXPERTS), pad_starts, 0),
                            axis=-1)
        dest_flat = (seg_start + rs_t).T.reshape(-1)
        n_used = (pad_ends[-1] // EXPERT_ROWS).astype(I32)
        blk = jnp.minimum(jnp.arange(n_blocks, dtype=I32), n_used - 1)
        block_e = jnp.minimum(
            jnp.sum(pad_ends[None, :] <= (blk * EXPERT_ROWS)[:, None], axis=1),
            N_EXPERTS - 1).astype(I32)

        n_used = n_used.reshape(1)
        last_blk = jnp.where(padded > 0, pad_ends // EXPERT_ROWS - 1, -1).astype(I32)
        xs = _dispatch_call(hp, dest_flat, last_blk, n_used, n_blocks, D // 2 // LANES)
        present = padded > 0
        eids = jnp.arange(N_EXPERTS, dtype=I32)
        suffix_min = lax.cummin(jnp.where(present, eids, N_EXPERTS)[::-1])[::-1]
        next_e = jnp.concatenate([suffix_min[1:], jnp.full((1,), N_EXPERTS, I32)])
        next_e = jnp.where(next_e >= N_EXPERTS, -1, next_e).astype(I32)
        slot_e = ((jnp.cumsum(present) - present) % 2).astype(I32)
        eo = _expert_call(xs, block_e, n_used, next_e, slot_e, w_gate, w_up, w_down, l, n_blocks)
        xf = _combine_call(dest_flat, wt_t.T, hp, xf, ada[l], ws_gate[l].astype(BF16),
                           ws_up[l].astype(BF16), ws_down[l].astype(BF16), final_norm, eo, S,
                           final=(l == L - 1))
    return xf.reshape(B, S, D)
```

```python
import functools

import jax
import jax.numpy as jnp
from jax import lax
from jax.experimental import pallas as pl
from jax.experimental.pallas import tpu as pltpu

F32 = jnp.float32
BF16 = jnp.bfloat16
U32 = jnp.uint32
I32 = jnp.int32

HEAD_DIM = 128
ROPE_DIM = HEAD_DIM // 4
ROPE_THETA = 500000.0
MOBA_BLOCK = 256
MOBA_TOPK = 3
CONV_LEN = 3
N_EXPERTS = 64
MOE_TOP_K = 8
N_GROUPS = 8
GROUP_SIZE = N_EXPERTS // N_GROUPS
TOPK_GROUPS = 4
ROUTED_SCALE = 2.5
NORM_EPS = 1e-6

VMEM_LIMIT_BYTES = 56 * 1024 * 1024
SUBLANES = 8
EXPERT_ROWS = 512
EXPERT_SUBBLOCKS = 1
ATTN_HEAD_GROUP = 8
ISSUE_UNROLL = 4
NEG_BIG = -1e30
LOG2_E = 1.4426950408889634
SUM_ROWS = 16

NT_DIMS = (((1,), (1,)), ((), ()))


def _tile(dim, preferred, align):
    if dim <= preferred:
        return dim
    t = preferred - preferred % align
    while t > align and dim % t:
        t -= align
    assert dim % t == 0, (dim, preferred, align)
    return t


def _cparams(n_axes):
    return pltpu.CompilerParams(
        dimension_semantics=("arbitrary",) * n_axes,
        vmem_limit_bytes=VMEM_LIMIT_BYTES)


def _bits(v):
    return lax.bitcast_convert_type(v, U32)


def _pack_halves(v):
    n = v.shape[1] // 2
    lo = _bits(v[:, :n].astype(BF16).astype(F32))
    hi = _bits(v[:, n:].astype(BF16).astype(F32))
    return hi | (lo >> 16)


def _unpack_lo(w):
    return lax.bitcast_convert_type(w << 16, F32)


def _unpack_hi(w):
    return lax.bitcast_convert_type(w & jnp.uint32(0xFFFF0000), F32)


LANES = 128


def _load_rows(ref, n, ch):
    return jnp.concatenate([ref[pl.ds(s, n, stride=ch), :] for s in range(ch)], axis=1)


def _store_rows(ref, val, n, ch):
    for s in range(ch):
        ref[pl.ds(s, n, stride=ch), :] = val[:, s * LANES:(s + 1) * LANES]


def _silu(v):
    return v * jax.nn.sigmoid(v)


def _rms_modulate(x, g, shift, scale):
    y = x * lax.rsqrt(jnp.mean(x * x, axis=-1, keepdims=True) + NORM_EPS)
    return (y * g) * (1.0 + scale) + shift


def _ada_kernel(c_ref, w_ref, b_ref, o_ref):
    ca = _silu(c_ref[...]).astype(BF16)
    o_ref[0] = jnp.dot(ca, w_ref[0].astype(BF16), preferred_element_type=F32) + b_ref[0]


def _ada_call(c, w_ada, b_ada):
    L, D, N = w_ada.shape
    B = c.shape[0]
    tn = _tile(N, 512, LANES)
    return pl.pallas_call(
        _ada_kernel,
        grid=(L, N // tn),
        in_specs=[
            pl.BlockSpec((B, D), lambda l, j: (0, 0)),
            pl.BlockSpec((1, D, tn), lambda l, j: (l, 0, j)),
            pl.BlockSpec((1, 1, tn), lambda l, j: (l, 0, j)),
        ],
        out_specs=pl.BlockSpec((1, B, tn), lambda l, j: (l, 0, j)),
        out_shape=jax.ShapeDtypeStruct((L, B, N), F32),
        compiler_params=_cparams(2),
        name="ada",
    )(c, w_ada, b_ada.reshape(L, 1, N))


def _inproj_kernel(x_ref, g_ref, ada_ref, w_ref, o_ref, h_sc):
    @pl.when(pl.program_id(1) == 0)
    def _():
        h = _rms_modulate(x_ref[...], g_ref[...], ada_ref[0:1, :], ada_ref[1:2, :])
        h_sc[...] = h.astype(BF16)

    o_ref[...] = jnp.dot(h_sc[...], w_ref[...], preferred_element_type=F32).astype(o_ref.dtype)


def _inproj_call(x, g, ada_l, w, S):
    T, D = x.shape
    N = w.shape[1]
    tm = _tile(S, 1024, SUBLANES)
    tn = _tile(N, 1024, LANES)
    return pl.pallas_call(
        _inproj_kernel,
        grid=(T // tm, N // tn),
        in_specs=[
            pl.BlockSpec((tm, D), lambda i, j: (i, 0)),
            pl.BlockSpec((1, D), lambda i, j: (0, 0)),
            pl.BlockSpec((None, 6, D), lambda i, j: ((i * tm) // S, 0, 0)),
            pl.BlockSpec((D, tn), lambda i, j: (0, j)),
        ],
        out_specs=pl.BlockSpec((tm, tn), lambda i, j: (i, j)),
        out_shape=jax.ShapeDtypeStruct((T, N), BF16),
        scratch_shapes=[pltpu.VMEM((tm, D), BF16)],
        compiler_params=_cparams(2),
        name="inproj",
    )(x, g.reshape(1, D), ada_l, w)


def _rope(v, c, s):
    half = ROPE_DIM // 2
    lane = lax.broadcasted_iota(I32, v.shape, 1)
    partner = jnp.where(lane < half,
                        pltpu.roll(v, HEAD_DIM - half, axis=1),
                        pltpu.roll(v, half, axis=1))
    return v * c + partner * s


def _attn_kernel(q_ref, k_ref, v_ref, cq_ref, sq_ref, ck_ref, sk_ref, o_ref,
                 kr_sc, vt_sc, km_sc, sel_sc, qs_sc, s_sc, s2_sc, m_sc, acc_sc, *, nb, hg):
    qi = pl.program_id(2)
    blk = MOBA_BLOCK
    dh = HEAD_DIM
    cols = lambda h: slice(h * dh, (h + 1) * dh)

    @pl.when(qi == 0)
    def _():
        km_sc[...] = jnp.zeros_like(km_sc)
        for j in range(nb):
            rows = pl.ds(j * blk, blk)
            ck = ck_ref[rows, :]
            sk = sk_ref[rows, :]
            for h in range(hg):
                kr = _rope(k_ref[rows, cols(h)].astype(F32), ck, sk)
                kr_sc[h, j] = kr.astype(BF16)
                km_sc[h, j:j + 1, :] = jnp.mean(kr, axis=0, keepdims=True)
                vt = v_ref[rows, cols(h)].astype(F32).T.astype(BF16)
                vt_sc[h, j] = jnp.concatenate([vt, jnp.ones((SUM_ROWS, blk), BF16)], axis=0)

    cq = cq_ref[...]
    sq = sq_ref[...]
    kidx = lax.broadcasted_iota(I32, (blk, blk), 0)
    qidx = lax.broadcasted_iota(I32, (blk, blk), 1)
    causal = kidx <= qidx
    for h in range(hg):
        q = _rope(q_ref[:, cols(h)].astype(F32), cq, sq)
        sel_sc[h] = lax.dot_general(km_sc[h], q, NT_DIMS, precision=lax.Precision.HIGHEST,
                                    preferred_element_type=F32)
        qs = (q * (HEAD_DIM ** -0.5 * LOG2_E)).astype(BF16)
        qs_sc[h] = qs
        st = lax.dot_general(kr_sc[h, qi], qs, NT_DIMS, preferred_element_type=F32)
        s_sc[h] = jnp.where(causal, st, NEG_BIG)

    for h in range(hg):
        gt = sel_sc[h]
        row = lax.broadcasted_iota(I32, gt.shape, 0)
        past = row < qi
        gm = jnp.where(past, gt, -jnp.inf)
        rank = jnp.zeros(gt.shape, F32)
        for jp in range(nb):
            gj = gm[jp:jp + 1, :]
            beats = (gj > gm) | ((gj == gm) & (jp < row))
            rank = rank + beats.astype(F32)
        sel_sc[h] = (past & (rank < MOBA_TOPK)).astype(F32)

    for h in range(hg):
        st = s_sc[h]
        m0 = jnp.max(st, axis=0, keepdims=True)
        p0 = jnp.exp2(st - m0)
        m_sc[h] = m0
        acc_sc[h] = jnp.dot(vt_sc[h, qi], p0.astype(BF16), preferred_element_type=F32)

    def stage_scores(j, buf):
        jc = jnp.minimum(j, nb - 1)
        for h in range(hg):
            buf[h] = lax.dot_general(kr_sc[h, jc], qs_sc[h], NT_DIMS, preferred_element_type=F32)

    def consume(j, buf):
        jc = jnp.minimum(j, nb - 1)
        for h in range(hg):
            sj = jnp.where(sel_sc[h, pl.ds(jc, 1), :] > 0.0, buf[h], NEG_BIG)
            m_old = m_sc[h]
            m_new = jnp.maximum(m_old, jnp.max(sj, axis=0, keepdims=True))
            alpha = jnp.exp2(m_old - m_new)
            p = jnp.exp2(sj - m_new)
            acc_sc[h] = alpha * acc_sc[h] + jnp.dot(vt_sc[h, jc], p.astype(BF16),
                                                    preferred_element_type=F32)
            m_sc[h] = m_new

    stage_scores(0, s_sc)

    def past_pair(jj, carry):
        a = 2 * jj
        stage_scores(a + 1, s2_sc)
        consume(a, s_sc)
        stage_scores(a + 2, s_sc)
        consume(a + 1, s2_sc)
        return carry

    lax.fori_loop(0, (qi + 1) // 2, past_pair, 0)

    for h in range(hg):
        o = acc_sc[h, :dh, :] / acc_sc[h, dh:dh + 1, :]
        o_ref[:, cols(h)] = o.T.astype(o_ref.dtype)


def _attn_call(proj, rope_c, rope_s, B, S, H):
    T = B * S
    blk = MOBA_BLOCK
    nb = S // blk
    nbp = -(-nb // SUBLANES) * SUBLANES
    dh = HEAD_DIM
    hg = min(ATTN_HEAD_GROUP, H)
    ng = H // hg
    w = hg * dh
    kern = functools.partial(_attn_kernel, nb=nb, hg=hg)
    return pl.pallas_call(
        kern,
        grid=(B, ng, nb),
        in_specs=[
            pl.BlockSpec((blk, w), lambda b, g, i: (b * nb + i, g)),
            pl.BlockSpec((S, w), lambda b, g, i: (b, ng + g)),
            pl.BlockSpec((S, w), lambda b, g, i: (b, 2 * ng + g)),
            pl.BlockSpec((blk, dh), lambda b, g, i: (b * nb + i, 0)),
            pl.BlockSpec((blk, dh), lambda b, g, i: (b * nb + i, 0)),
            pl.BlockSpec((S, dh), lambda b, g, i: (b, 0)),
            pl.BlockSpec((S, dh), lambda b, g, i: (b, 0)),
        ],
        out_specs=pl.BlockSpec((blk, w), lambda b, g, i: (b * nb + i, g)),
        out_shape=jax.ShapeDtypeStruct((T, H * dh), BF16),
        scratch_shapes=[
            pltpu.VMEM((hg, nb, blk, dh), BF16),
            pltpu.VMEM((hg, nb, dh + SUM_ROWS, blk), BF16),
            pltpu.VMEM((hg, nbp, dh), F32),
            pltpu.VMEM((hg, nbp, blk), F32),
            pltpu.VMEM((hg, blk, dh), BF16),
            pltpu.VMEM((hg, blk, blk), F32),
            pltpu.VMEM((hg, blk, blk), F32),
            pltpu.VMEM((hg, 1, blk), F32),
            pltpu.VMEM((hg, dh + SUM_ROWS, blk), F32),
        ],
        compiler_params=_cparams(3),
        name="moba_attn",
    )(proj, proj, proj, rope_c, rope_s, rope_c, rope_s)


def _mix1_kernel(ya_ref, cb_ref, cc_ref, cx_ref, hcc_ref, hcx_ref, ga_ref, gc_ref, cw_ref,
                 wb0_ref, wb1_ref, o_ref, *, tm, S):
    i = pl.program_id(0)
    u = cc_ref[...].astype(F32) * cx_ref[...].astype(F32)
    hu = hcc_ref[...].astype(F32) * hcx_ref[...].astype(F32)
    seq_start = (i * tm) % S == 0
    hu = jnp.where(seq_start, 0.0, hu)
    h1 = hu[SUBLANES - 1:SUBLANES, :]
    h2 = hu[SUBLANES - 2:SUBLANES - 1, :]
    r = lax.broadcasted_iota(I32, u.shape, 0)
    u1 = jnp.where(r == 0, h1, pltpu.roll(u, 1, axis=0))
    u2 = jnp.where(r == 0, h2, jnp.where(r == 1, h1, pltpu.roll(u, 2, axis=0)))
    conv = cw_ref[0:1, :] * u2 + cw_ref[1:2, :] * u1 + cw_ref[2:3, :] * u
    yc = (cb_ref[...].astype(F32) * conv).astype(BF16)
    a = jnp.dot(ya_ref[...], wb0_ref[...], preferred_element_type=F32)
    c = jnp.dot(yc, wb1_ref[...], preferred_element_type=F32)
    merged = (jax.nn.sigmoid(ga_ref[...].astype(F32)) * a
              + jax.nn.sigmoid(gc_ref[...].astype(F32)) * c)
    o_ref[...] = merged.astype(o_ref.dtype)


def _mix1_call(ya, proj, conv_w, wb0, wb1, S, D):
    T, A = ya.shape
    assert conv_w.shape == (CONV_LEN, A) and (6 * A) % D == 0
    tm = _tile(S, 512, LANES)
    gcol = 6 * A // D
    hb = tm // SUBLANES
    kern = functools.partial(_mix1_kernel, tm=tm, S=S)
    halo = lambda col: pl.BlockSpec((SUBLANES, A), lambda i: (jnp.maximum(i * hb - 1, 0), col))
    return pl.pallas_call(
        kern,
        grid=(T // tm,),
        in_specs=[
            pl.BlockSpec((tm, A), lambda i: (i, 0)),
            pl.BlockSpec((tm, A), lambda i: (i, 3)),
            pl.BlockSpec((tm, A), lambda i: (i, 4)),
            pl.BlockSpec((tm, A), lambda i: (i, 5)),
            halo(4),
            halo(5),
            pl.BlockSpec((tm, D), lambda i: (i, gcol)),
            pl.BlockSpec((tm, D), lambda i: (i, gcol + 1)),
            pl.BlockSpec((CONV_LEN, A), lambda i: (0, 0)),
            pl.BlockSpec((A, D), lambda i: (0, 0)),
            pl.BlockSpec((A, D), lambda i: (0, 0)),
        ],
        out_specs=pl.BlockSpec((tm, D), lambda i: (i, 0)),
        out_shape=jax.ShapeDtypeStruct((T, D), BF16),
        compiler_params=_cparams(1),
        name="mix_merge",
    )(ya, proj, proj, proj, proj, proj, proj, proj, conv_w, wb0, wb1)


def _mix2_kernel(m_ref, wo_ref, x_ref, ada_ref, g_ref, wr_ref, xo_ref, hp_ref, lg_ref):
    out = jnp.dot(m_ref[...], wo_ref[...], preferred_element_type=F32)
    x = x_ref[...] + ada_ref[2:3, :] * out
    xo_ref[...] = x
    h = _rms_modulate(x, g_ref[...], ada_ref[3:4, :], ada_ref[4:5, :])
    tm, d = x.shape
    _store_rows(hp_ref, _pack_halves(h), tm, d // 2 // LANES)
    e = wr_ref.shape[1] // 2
    h_hi = h.astype(BF16)
    h_lo = (h - h_hi.astype(F32)).astype(BF16)
    lg2 = jnp.dot(h_hi, wr_ref[...], preferred_element_type=F32)
    lg1 = jnp.dot(h_lo, wr_ref[...], preferred_element_type=F32)
    lg = lg2[:, :e] + lg2[:, e:] + lg1[:, :e]
    lg_ref[...] = lg.T


def _mix2_call(merged, w_out, x, ada_l, g, wr, S):
    T, D = x.shape
    tm = _tile(S, 512, LANES)
    E = wr.shape[1] // 2
    ch = D // 2 // LANES
    return pl.pallas_call(
        _mix2_kernel,
        grid=(T // tm,),
        in_specs=[
            pl.BlockSpec((tm, D), lambda i: (i, 0)),
            pl.BlockSpec((D, D), lambda i: (0, 0)),
            pl.BlockSpec((tm, D), lambda i: (i, 0)),
            pl.BlockSpec((None, 6, D), lambda i: ((i * tm) // S, 0, 0)),
            pl.BlockSpec((1, D), lambda i: (0, 0)),
            pl.BlockSpec((D, 2 * E), lambda i: (0, 0)),
        ],
        out_specs=[
            pl.BlockSpec((tm, D), lambda i: (i, 0)),
            pl.BlockSpec((tm * ch, LANES), lambda i: (i, 0)),
            pl.BlockSpec((E, tm), lambda i: (0, i)),
        ],
        out_shape=[
            jax.ShapeDtypeStruct((T, D), F32),
            jax.ShapeDtypeStruct((T * ch, LANES), U32),
            jax.ShapeDtypeStruct((E, T), F32),
        ],
        compiler_params=_cparams(1),
        name="outproj_norm_router",
    )(merged, w_out, x, ada_l, g.reshape(1, D), wr)


def _route_kernel(lg_ref, bias_ref, tri_ref, idx_ref, wt_ref, rs_ref, cnt_ref, carry_sc):
    step = pl.program_id(0)

    @pl.when(step == 0)
    def _():
        carry_sc[...] = jnp.zeros_like(carry_sc)

    G = N_GROUPS
    scores = jax.nn.sigmoid(lg_ref[...])
    biased = scores + bias_ref[...]
    sc = [scores[i * G:(i + 1) * G, :] for i in range(GROUP_SIZE)]
    bi = [biased[i * G:(i + 1) * G, :] for i in range(GROUP_SIZE)]
    grow = lax.broadcasted_iota(I32, bi[0].shape, 0)
    eid = [grow * GROUP_SIZE + i for i in range(GROUP_SIZE)]
    neg_inf = jnp.float32(-jnp.inf)

    m1 = functools.reduce(jnp.maximum, bi)
    first = jnp.full(grow.shape, GROUP_SIZE, I32)
    for i in reversed(range(GROUP_SIZE)):
        first = jnp.where(bi[i] == m1, i, first)
    m2 = functools.reduce(jnp.maximum,
                          [jnp.where(first == i, neg_inf, bi[i]) for i in range(GROUP_SIZE)])
    gs = m1 + m2

    grank = jnp.zeros(gs.shape, F32)
    for gp in range(G):
        v = gs[gp:gp + 1, :]
        grank = grank + ((v > gs) | ((v == gs) & (gp < grow))).astype(F32)
    gsel = grank < TOPK_GROUPS

    ms = [jnp.where(gsel, b, neg_inf) for b in bi]
    chosen = [jnp.zeros(grow.shape, jnp.bool_) for _ in range(GROUP_SIZE)]
    picked_idx = []
    picked_score = []
    for _ in range(MOE_TOP_K):
        mx = jnp.max(functools.reduce(jnp.maximum, ms), axis=0, keepdims=True)
        cand = functools.reduce(
            jnp.minimum, [jnp.where(ms[i] == mx, eid[i], N_EXPERTS) for i in range(GROUP_SIZE)])
        ce = jnp.min(cand, axis=0, keepdims=True)
        hit = [eid[i] == ce for i in range(GROUP_SIZE)]
        s_k = jnp.sum(functools.reduce(
            jnp.add, [jnp.where(hit[i], sc[i], 0.0) for i in range(GROUP_SIZE)]),
            axis=0, keepdims=True)
        ms = [jnp.where(hit[i], neg_inf, ms[i]) for i in range(GROUP_SIZE)]
        chosen = [chosen[i] | hit[i] for i in range(GROUP_SIZE)]
        picked_idx.append(ce)
        picked_score.append(s_k)

    wsum = functools.reduce(jnp.add, picked_score)
    for k in range(MOE_TOP_K):
        idx_ref[k:k + 1, :] = picked_idx[k]
        wt_ref[k:k + 1, :] = picked_score[k] / wsum * ROUTED_SCALE

    selm = jnp.concatenate([c.astype(F32) for c in chosen], axis=0)
    pre = jnp.dot(selm.astype(BF16), tri_ref[...], preferred_element_type=F32) + carry_sc[:, 0:1]
    pre_slab = [pre[i * G:(i + 1) * G, :] for i in range(GROUP_SIZE)]
    for k in range(MOE_TOP_K):
        rk = jnp.sum(functools.reduce(
            jnp.add, [jnp.where(eid[i] == picked_idx[k], pre_slab[i], 0.0)
                      for i in range(GROUP_SIZE)]), axis=0, keepdims=True)
        rs_ref[k:k + 1, :] = rk.astype(I32)
    carry_sc[...] = carry_sc[...] + jnp.sum(selm, axis=1, keepdims=True)
    cnt_ref[...] = carry_sc[...]


def _route_call(lgt, bias_perm, T):
    E = lgt.shape[0]
    tr = _tile(T, 512, LANES)
    tri = (lax.broadcasted_iota(I32, (tr, tr), 0) < lax.broadcasted_iota(I32, (tr, tr), 1)).astype(BF16)
    return pl.pallas_call(
        _route_kernel,
        grid=(T // tr,),
        in_specs=[
            pl.BlockSpec((E, tr), lambda i: (0, i)),
            pl.BlockSpec((E, 1), lambda i: (0, 0)),
            pl.BlockSpec((tr, tr), lambda i: (0, 0)),
        ],
        out_specs=[
            pl.BlockSpec((MOE_TOP_K, tr), lambda i: (0, i)),
            pl.BlockSpec((MOE_TOP_K, tr), lambda i: (0, i)),
            pl.BlockSpec((MOE_TOP_K, tr), lambda i: (0, i)),
            pl.BlockSpec((E, 128), lambda i: (0, 0)),
        ],
        out_shape=[
            jax.ShapeDtypeStruct((MOE_TOP_K, T), I32),
            jax.ShapeDtypeStruct((MOE_TOP_K, T), F32),
            jax.ShapeDtypeStruct((MOE_TOP_K, T), I32),
            jax.ShapeDtypeStruct((E, 128), F32),
        ],
        scratch_shapes=[pltpu.VMEM((E, 128), F32)],
        compiler_params=_cparams(1),
        name="route",
    )(lgt, bias_perm.reshape(E, 1), tri)


def _dispatch_kernel(zs_ref, nu_ref, hp_ref, dest_ref, xs_ref, z_sc, sem, *, td, ch, n_blocks):
    i = pl.program_id(0)
    zrows = EXPERT_ROWS * ch

    @pl.when(i == 0)
    def _():
        z_sc[...] = jnp.zeros_like(z_sc)
        zero_wait = pltpu.make_async_copy(z_sc, xs_ref.at[pl.ds(0, zrows), :], sem)

        def zero_block(b, carry):
            b0 = pl.multiple_of(b * zrows, zrows)
            pltpu.make_async_copy(z_sc, xs_ref.at[pl.ds(b0, zrows), :], sem).start()
            return carry

        def wait_block(b, carry):
            zero_wait.wait()
            return carry

        for e in range(N_EXPERTS):
            @pl.when(zs_ref[e] >= 0)
            def _(e=e):
                zero_block(zs_ref[e], 0)
        lax.fori_loop(nu_ref[0], n_blocks, zero_block, 0)
        for e in range(N_EXPERTS):
            @pl.when(zs_ref[e] >= 0)
            def _():
                zero_wait.wait()
        lax.fori_loop(nu_ref[0], n_blocks, wait_block, 0)

    def issue(tp, carry):
        for u in range(ISSUE_UNROLL):
            t = tp * ISSUE_UNROLL + u
            src = hp_ref.at[pl.ds(pl.multiple_of(t * ch, ch), ch), :]
            for k in range(MOE_TOP_K):
                r0 = pl.multiple_of(dest_ref[t * MOE_TOP_K + k] * ch, ch)
                pltpu.make_async_copy(src, xs_ref.at[pl.ds(r0, ch), :], sem).start(priority=k % 2)
        return carry

    lax.fori_loop(0, td // ISSUE_UNROLL, issue, 0)
    for k in range(MOE_TOP_K):
        pltpu.make_async_copy(hp_ref, xs_ref.at[pl.ds(0, td * ch), :], sem).wait()


def _dispatch_call(hp, dest_flat, zstart, n_used, n_blocks, ch):
    T = hp.shape[0] // ch
    td = _tile(T, 256, SUBLANES * ISSUE_UNROLL)
    n_rows = n_blocks * EXPERT_ROWS
    kern = functools.partial(_dispatch_kernel, td=td, ch=ch, n_blocks=n_blocks)
    return pl.pallas_call(
        kern,
        grid_spec=pltpu.PrefetchScalarGridSpec(
            num_scalar_prefetch=2,
            grid=(T // td,),
            in_specs=[
                pl.BlockSpec((td * ch, LANES), lambda i, zs, nu: (i, 0)),
                pl.BlockSpec((td * MOE_TOP_K,), lambda i, zs, nu: (i,), memory_space=pltpu.SMEM),
            ],
            out_specs=pl.BlockSpec(memory_space=pl.ANY),
            scratch_shapes=[pltpu.VMEM((EXPERT_ROWS * ch, LANES), U32), pltpu.SemaphoreType.DMA],
        ),
        out_shape=jax.ShapeDtypeStruct((n_rows * ch, LANES), U32),
        compiler_params=_cparams(1),
        name="dispatch",
    )(zstart, n_used, hp, dest_flat)


def _expert_kernel(be_ref, nu_ref, nx_ref, sl_ref, xs_ref, wg_hbm, wu_hbm, wd_hbm, o_ref,
                   wg_st, wu_st, wd_st, wg_sc, wu_sc, wd_sc, sems, *, layer):
    i = pl.program_id(0)
    e = be_ref[i]
    prev = be_ref[jnp.maximum(i - 1, 0)]
    changed = (i == 0) | (e != prev)

    def weight_copies(ex, slot):
        return (pltpu.make_async_copy(wg_hbm.at[layer, ex], wg_st.at[slot], sems.at[slot, 0]),
                pltpu.make_async_copy(wu_hbm.at[layer, ex], wu_st.at[slot], sems.at[slot, 1]),
                pltpu.make_async_copy(wd_hbm.at[layer, ex], wd_st.at[slot], sems.at[slot, 2]))

    @pl.when(i == 0)
    def _():
        for cp in weight_copies(e, sl_ref[e]):
            cp.start(priority=1)

    @pl.when(changed)
    def _():
        slot = sl_ref[e]
        for cp in weight_copies(e, slot):
            cp.wait()
        wg_sc[...] = wg_st[slot].astype(BF16)
        wu_sc[...] = wu_st[slot].astype(BF16)
        wd_sc[...] = wd_st[slot].astype(BF16)
        nxt = nx_ref[e]

        @pl.when(nxt >= 0)
        def _():
            for cp in weight_copies(nxt, 1 - slot):
                cp.start(priority=1)

    @pl.when(i < nu_ref[0])
    def _():
        ch = wg_sc.shape[0] // 2 // LANES
        sub = EXPERT_ROWS // EXPERT_SUBBLOCKS
        gu = []
        for s in range(EXPERT_SUBBLOCKS):
            w = _load_rows(xs_ref.at[pl.ds(s * sub * ch, sub * ch), :], sub, ch)
            x = jnp.concatenate([_unpack_lo(w).astype(BF16), _unpack_hi(w).astype(BF16)], axis=1)
            gu.append((jnp.dot(x, wg_sc[...], preferred_element_type=F32),
                       jnp.dot(x, wu_sc[...], preferred_element_type=F32)))
        for s in range(EXPERT_SUBBLOCKS):
            gate, up = gu[s]
            act = (_silu(gate) * up).astype(BF16)
            out = jnp.dot(act, wd_sc[...], preferred_element_type=F32)
            _store_rows(o_ref.at[pl.ds(s * sub * ch, sub * ch), :], _pack_halves(out), sub, ch)

    @pl.when(i >= nu_ref[0])
    def _():
        o_ref[...] = jnp.zeros_like(o_ref)


def _expert_call(xs, block_e, n_used, next_e, slot_e, w_gate, w_up, w_down, layer, n_blocks):
    _, E, D, F = w_gate.shape
    ch = D // 2 // LANES
    rows = lambda i, be, nu, nx, sl: (jnp.minimum(i, nu[0] - 1), 0)
    out_rows = lambda i, be, nu, nx, sl: (i, 0)
    kern = functools.partial(_expert_kernel, layer=layer)
    return pl.pallas_call(
        kern,
        grid_spec=pltpu.PrefetchScalarGridSpec(
            num_scalar_prefetch=4,
            grid=(n_blocks,),
            in_specs=[
                pl.BlockSpec((EXPERT_ROWS * ch, LANES), rows),
                pl.BlockSpec(memory_space=pl.ANY),
                pl.BlockSpec(memory_space=pl.ANY),
                pl.BlockSpec(memory_space=pl.ANY),
            ],
            out_specs=pl.BlockSpec((EXPERT_ROWS * ch, LANES), out_rows),
            scratch_shapes=[
                pltpu.VMEM((2, D, F), F32), pltpu.VMEM((2, D, F), F32), pltpu.VMEM((2, F, D), F32),
                pltpu.VMEM((D, F), BF16), pltpu.VMEM((D, F), BF16), pltpu.VMEM((F, D), BF16),
                pltpu.SemaphoreType.DMA((2, 3)),
            ],
        ),
        out_shape=jax.ShapeDtypeStruct((n_blocks * EXPERT_ROWS * ch, LANES), U32),
        compiler_params=_cparams(1),
        name="experts",
    )(block_e, n_used, next_e, slot_e, xs, w_gate, w_up, w_down)


def _combine_kernel(dest_ref, wt_ref, hp_ref, x_ref, ada_ref, wsg_ref, wsu_ref, wsd_ref, fn_ref,
                    eo_ref, xo_ref, gbuf, sem, *, tc, final):
    half = x_ref.shape[1] // 2
    ch = half // LANES

    def issue(tp, carry):
        for u in range(ISSUE_UNROLL):
            t = tp * ISSUE_UNROLL + u
            t0 = pl.multiple_of(t * ch, ch)
            for k in range(MOE_TOP_K):
                r0 = pl.multiple_of(dest_ref[t * MOE_TOP_K + k] * ch, ch)
                pltpu.make_async_copy(eo_ref.at[pl.ds(r0, ch), :], gbuf.at[k, pl.ds(t0, ch), :],
                                      sem).start(priority=k % 2)
        return carry

    lax.fori_loop(0, tc // ISSUE_UNROLL, issue, 0)

    w = _load_rows(hp_ref, tc, ch)
    h = jnp.concatenate([_unpack_lo(w).astype(BF16), _unpack_hi(w).astype(BF16)], axis=1)
    gate = jnp.dot(h, wsg_ref[...], preferred_element_type=F32)
    up = jnp.dot(h, wsu_ref[...], preferred_element_type=F32)
    act = (_silu(gate) * up).astype(BF16)
    shared = jnp.dot(act, wsd_ref[...], preferred_element_type=F32)

    for k in range(MOE_TOP_K):
        pltpu.make_async_copy(eo_ref.at[pl.ds(0, tc * ch), :], gbuf.at[k], sem).wait()

    wts = wt_ref[...]
    ylo = shared[:, :half]
    yhi = shared[:, half:]
    for k in range(MOE_TOP_K):
        g = _load_rows(gbuf.at[k], tc, ch)
        wk = wts[:, k:k + 1]
        ylo = ylo + wk * _unpack_lo(g)
        yhi = yhi + wk * _unpack_hi(g)
    gf = ada_ref[5:6, :]
    xlo = x_ref[:, :half] + gf[:, :half] * ylo
    xhi = x_ref[:, half:] + gf[:, half:] * yhi
    if final:
        ss = (jnp.sum(xlo * xlo, axis=-1, keepdims=True)
              + jnp.sum(xhi * xhi, axis=-1, keepdims=True))
        inv = lax.rsqrt(ss / (2 * half) + NORM_EPS)
        xlo = xlo * inv * fn_ref[:, :half]
        xhi = xhi * inv * fn_ref[:, half:]
    xo_ref[:, :half] = xlo
    xo_ref[:, half:] = xhi


def _combine_call(dest_flat, wts, hp, x, ada_l, wsg, wsu, wsd, fnorm, eo, S, final):
    T, D = x.shape
    ch = D // 2 // LANES
    Fs = wsg.shape[1]
    tc = _tile(S, 256, SUBLANES * ISSUE_UNROLL)
    kern = functools.partial(_combine_kernel, tc=tc, final=final)
    return pl.pallas_call(
        kern,
        grid=(T // tc,),
        in_specs=[
            pl.BlockSpec((tc * MOE_TOP_K,), lambda i: (i,), memory_space=pltpu.SMEM),
            pl.BlockSpec((tc, MOE_TOP_K), lambda i: (i, 0)),
            pl.BlockSpec((tc * ch, LANES), lambda i: (i, 0)),
            pl.BlockSpec((tc, D), lambda i: (i, 0)),
            pl.BlockSpec((None, 6, D), lambda i: ((i * tc) // S, 0, 0)),
            pl.BlockSpec((D, Fs), lambda i: (0, 0)),
            pl.BlockSpec((D, Fs), lambda i: (0, 0)),
            pl.BlockSpec((Fs, D), lambda i: (0, 0)),
            pl.BlockSpec((1, D), lambda i: (0, 0)),
            pl.BlockSpec(memory_space=pl.ANY),
        ],
        out_specs=pl.BlockSpec((tc, D), lambda i: (i, 0)),
        out_shape=jax.ShapeDtypeStruct((T, D), F32),
        scratch_shapes=[pltpu.VMEM((MOE_TOP_K, tc * ch, LANES), U32), pltpu.SemaphoreType.DMA],
        compiler_params=_cparams(1),
        name="combine_shared",
    )(dest_flat, wts, hp, x, ada_l, wsg, wsu, wsd, fnorm.reshape(1, D), eo)


def _rope_tables(positions):
    half = ROPE_DIM // 2
    inv_freq = ROPE_THETA ** (-jnp.arange(0, ROPE_DIM, 2, dtype=F32) / ROPE_DIM)
    ang = positions.astype(F32).reshape(-1, 1) * inv_freq
    cos, sin = jnp.cos(ang), jnp.sin(ang)
    T = ang.shape[0]
    rest = HEAD_DIM - ROPE_DIM
    c = jnp.concatenate([cos, cos, jnp.ones((T, rest), F32)], axis=-1)
    s = jnp.concatenate([-sin, sin, jnp.zeros((T, rest), F32)], axis=-1)
    return c, s


def _expert_row_perm():
    r = jnp.arange(N_EXPERTS)
    return (r % N_GROUPS) * GROUP_SIZE + r // N_GROUPS


def kernel(x, c, positions, norm_mix, w_ada, b_ada, w_in, conv_w, w_branch, w_out, norm_ffn, w_router, router_bias, w_gate, w_up, w_down, ws_gate, ws_up, ws_down, final_norm):
    B, S, D = x.shape
    L = w_ada.shape[0]
    T = B * S
    A = w_branch.shape[2]
    H = A // HEAD_DIM
    assert S % MOBA_BLOCK == 0 and w_in.shape[2] == 6 * A + 2 * D

    n_assign = T * MOE_TOP_K
    n_blocks = -(-(n_assign + N_EXPERTS * (EXPERT_ROWS - 1)) // EXPERT_ROWS)

    rope_c, rope_s = _rope_tables(positions)
    ada = _ada_call(c, w_ada, b_ada).reshape(L, B, 6, D)
    perm = _expert_row_perm()
    xf = x.reshape(T, D)

    for l in range(L):
        proj = _inproj_call(xf, norm_mix[l], ada[l], w_in[l].astype(BF16), S)
        ya = _attn_call(proj, rope_c, rope_s, B, S, H)
        merged = _mix1_call(ya, proj, conv_w[l], w_branch[l, 0].astype(BF16),
                            w_branch[l, 1].astype(BF16), S, D)
        wr = w_router[l][:, perm]
        wr_hi = wr.astype(BF16)
        wr_lo = (wr - wr_hi.astype(F32)).astype(BF16)
        xf, hp, lgt = _mix2_call(merged, w_out[l].astype(BF16), xf, ada[l], norm_ffn[l],
                                 jnp.concatenate([wr_hi, wr_lo], axis=1), S)

        idx_t, wt_t, rs_t, cnt = _route_call(lgt, router_bias[l][perm], T)
        counts = cnt[perm, 0].astype(I32)
        padded = (counts + EXPERT_ROWS - 1) // EXPERT_ROWS * EXPERT_ROWS
        pad_ends = jnp.cumsum(padded)
        pad_starts = pad_ends - padded
        seg_start = jnp.sum(jnp.where(idx_t[:, :, None] == jnp.arange(N_EXPERTS), pad_starts, 0),
                            axis=-1)
        dest_flat = (seg_start + rs_t).T.reshape(-1)
        n_used = (pad_ends[-1] // EXPERT_ROWS).astype(I32)
        blk = jnp.minimum(jnp.arange(n_blocks, dtype=I32), n_used - 1)
        block_e = jnp.minimum(
            jnp.sum(pad_ends[None, :] <= (blk * EXPERT_ROWS)[:, None], axis=1),
            N_EXPERTS - 1).astype(I32)

        n_used = n_used.reshape(1)
        last_blk = jnp.where(padded > 0, pad_ends // EXPERT_ROWS - 1, -1).astype(I32)
        xs = _dispatch_call(hp, dest_flat, last_blk, n_used, n_blocks, D // 2 // LANES)
        present = padded > 0
        eids = jnp.arange(N_EXPERTS, dtype=I32)
        suffix_min = lax.cummin(jnp.where(present, eids, N_EXPERTS)[::-1])[::-1]
        next_e = jnp.concatenate([suffix_min[1:], jnp.full((1,), N_EXPERTS, I32)])
        next_e = jnp.where(next_e >= N_EXPERTS, -1, next_e).astype(I32)
        slot_e = ((jnp.cumsum(present) - present) % 2).astype(I32)
        eo = _expert_call(xs, block_e, n_used, next_e, slot_e, w_gate, w_up, w_down, l, n_blocks)
        xf = _combine_call(dest_flat, wt_t.T, hp, xf, ada[l], ws_gate[l].astype(BF16),
                           ws_up[l].astype(BF16), ws_down[l].astype(BF16), final_norm, eo, S,
                           final=(l == L - 1))
    return xf.reshape(B, S, D)
```

```python
import functools

import jax
import jax.numpy as jnp
from jax import lax
from jax.experimental import pallas as pl
from jax.experimental.pallas import tpu as pltpu

F32 = jnp.float32
BF16 = jnp.bfloat16
U32 = jnp.uint32
I32 = jnp.int32

HEAD_DIM = 128
ROPE_DIM = HEAD_DIM // 4
ROPE_THETA = 500000.0
MOBA_BLOCK = 256
MOBA_TOPK = 3
CONV_LEN = 3
N_EXPERTS = 64
MOE_TOP_K = 8
N_GROUPS = 8
GROUP_SIZE = N_EXPERTS // N_GROUPS
TOPK_GROUPS = 4
ROUTED_SCALE = 2.5
NORM_EPS = 1e-6

VMEM_LIMIT_BYTES = 56 * 1024 * 1024
SUBLANES = 8
EXPERT_ROWS = 512
ATTN_HEAD_GROUP = 8
ISSUE_UNROLL = 4
NEG_BIG = -1e30
LOG2_E = 1.4426950408889634
SUM_ROWS = 16

NT_DIMS = (((1,), (1,)), ((), ()))


def _tile(dim, preferred, align):
    if dim <= preferred:
        return dim
    t = preferred - preferred % align
    while t > align and dim % t:
        t -= align
    assert dim % t == 0, (dim, preferred, align)
    return t


def _cparams(n_axes):
    return pltpu.CompilerParams(
        dimension_semantics=("arbitrary",) * n_axes,
        vmem_limit_bytes=VMEM_LIMIT_BYTES)


def _bits(v):
    return lax.bitcast_convert_type(v, U32)


def _pack_halves(v):
    n = v.shape[1] // 2
    lo = _bits(v[:, :n].astype(BF16).astype(F32))
    hi = _bits(v[:, n:].astype(BF16).astype(F32))
    return hi | (lo >> 16)


def _unpack_lo(w):
    return lax.bitcast_convert_type(w << 16, F32)


def _unpack_hi(w):
    return lax.bitcast_convert_type(w & jnp.uint32(0xFFFF0000), F32)


LANES = 128


def _load_rows(ref, n, ch):
    return jnp.concatenate([ref[pl.ds(s, n, stride=ch), :] for s in range(ch)], axis=1)


def _store_rows(ref, val, n, ch):
    for s in range(ch):
        ref[pl.ds(s, n, stride=ch), :] = val[:, s * LANES:(s + 1) * LANES]


def _silu(v):
    return v * jax.nn.sigmoid(v)


def _rms_modulate(x, g, shift, scale):
    y = x * lax.rsqrt(jnp.mean(x * x, axis=-1, keepdims=True) + NORM_EPS)
    return (y * g) * (1.0 + scale) + shift


def _ada_kernel(c_ref, w_ref, b_ref, o_ref):
    ca = _silu(c_ref[...]).astype(BF16)
    o_ref[0] = jnp.dot(ca, w_ref[0].astype(BF16), preferred_element_type=F32) + b_ref[0]


def _ada_call(c, w_ada, b_ada):
    L, D, N = w_ada.shape
    B = c.shape[0]
    tn = _tile(N, 512, LANES)
    return pl.pallas_call(
        _ada_kernel,
        grid=(L, N // tn),
        in_specs=[
            pl.BlockSpec((B, D), lambda l, j: (0, 0)),
            pl.BlockSpec((1, D, tn), lambda l, j: (l, 0, j)),
            pl.BlockSpec((1, 1, tn), lambda l, j: (l, 0, j)),
        ],
        out_specs=pl.BlockSpec((1, B, tn), lambda l, j: (l, 0, j)),
        out_shape=jax.ShapeDtypeStruct((L, B, N), F32),
        compiler_params=_cparams(2),
        name="ada",
    )(c, w_ada, b_ada.reshape(L, 1, N))


def _inproj_kernel(x_ref, g_ref, ada_ref, w_ref, o_ref, h_sc):
    @pl.when(pl.program_id(1) == 0)
    def _():
        h = _rms_modulate(x_ref[...], g_ref[...], ada_ref[0:1, :], ada_ref[1:2, :])
        h_sc[...] = h.astype(BF16)

    o_ref[...] = jnp.dot(h_sc[...], w_ref[...], preferred_element_type=F32).astype(o_ref.dtype)


def _inproj_call(x, g, ada_l, w, S):
    T, D = x.shape
    N = w.shape[1]
    tm = _tile(S, 1024, SUBLANES)
    tn = _tile(N, 1024, LANES)
    return pl.pallas_call(
        _inproj_kernel,
        grid=(T // tm, N // tn),
        in_specs=[
            pl.BlockSpec((tm, D), lambda i, j: (i, 0)),
            pl.BlockSpec((1, D), lambda i, j: (0, 0)),
            pl.BlockSpec((None, 6, D), lambda i, j: ((i * tm) // S, 0, 0)),
            pl.BlockSpec((D, tn), lambda i, j: (0, j)),
        ],
        out_specs=pl.BlockSpec((tm, tn), lambda i, j: (i, j)),
        out_shape=jax.ShapeDtypeStruct((T, N), BF16),
        scratch_shapes=[pltpu.VMEM((tm, D), BF16)],
        compiler_params=_cparams(2),
        name="inproj",
    )(x, g.reshape(1, D), ada_l, w)


def _rope(v, c, s):
    half = ROPE_DIM // 2
    lane = lax.broadcasted_iota(I32, v.shape, 1)
    partner = jnp.where(lane < half,
                        pltpu.roll(v, HEAD_DIM - half, axis=1),
                        pltpu.roll(v, half, axis=1))
    return v * c + partner * s


def _attn_kernel(q_ref, k_ref, v_ref, cq_ref, sq_ref, ck_ref, sk_ref, o_ref,
                 kr_sc, vt_sc, km_sc, sel_sc, qs_sc, s_sc, s2_sc, m_sc, acc_sc, *, nb, hg):
    qi = pl.program_id(2)
    blk = MOBA_BLOCK
    dh = HEAD_DIM
    cols = lambda h: slice(h * dh, (h + 1) * dh)

    @pl.when(qi == 0)
    def _():
        km_sc[...] = jnp.zeros_like(km_sc)
        for j in range(nb):
            rows = pl.ds(j * blk, blk)
            ck = ck_ref[rows, :]
            sk = sk_ref[rows, :]
            for h in range(hg):
                kr = _rope(k_ref[rows, cols(h)].astype(F32), ck, sk)
                kr_sc[h, j] = kr.astype(BF16)
                km_sc[h, j:j + 1, :] = jnp.mean(kr, axis=0, keepdims=True)
                vt = v_ref[rows, cols(h)].astype(F32).T.astype(BF16)
                vt_sc[h, j] = jnp.concatenate([vt, jnp.ones((SUM_ROWS, blk), BF16)], axis=0)

    cq = cq_ref[...]
    sq = sq_ref[...]
    kidx = lax.broadcasted_iota(I32, (blk, blk), 0)
    qidx = lax.broadcasted_iota(I32, (blk, blk), 1)
    causal = kidx <= qidx
    for h in range(hg):
        q = _rope(q_ref[:, cols(h)].astype(F32), cq, sq)
        sel_sc[h] = lax.dot_general(km_sc[h], q, NT_DIMS, precision=lax.Precision.HIGHEST,
                                    preferred_element_type=F32)
        qs = (q * (HEAD_DIM ** -0.5 * LOG2_E)).astype(BF16)
        qs_sc[h] = qs
        st = lax.dot_general(kr_sc[h, qi], qs, NT_DIMS, preferred_element_type=F32)
        s_sc[h] = jnp.where(causal, st, NEG_BIG)

    for h in range(hg):
        gt = sel_sc[h]
        row = lax.broadcasted_iota(I32, gt.shape, 0)
        past = row < qi
        gm = jnp.where(past, gt, -jnp.inf)
        rank = jnp.zeros(gt.shape, F32)
        for jp in range(nb):
            gj = gm[jp:jp + 1, :]
            beats = (gj > gm) | ((gj == gm) & (jp < row))
            rank = rank + beats.astype(F32)
        sel_sc[h] = (past & (rank < MOBA_TOPK)).astype(F32)

    for h in range(hg):
        st = s_sc[h]
        m0 = jnp.max(st, axis=0, keepdims=True)
        p0 = jnp.exp2(st - m0)
        m_sc[h] = m0
        acc_sc[h] = jnp.dot(vt_sc[h, qi], p0.astype(BF16), preferred_element_type=F32)

    def stage_scores(j, buf):
        jc = jnp.minimum(j, nb - 1)
        for h in range(hg):
            buf[h] = lax.dot_general(kr_sc[h, jc], qs_sc[h], NT_DIMS, preferred_element_type=F32)

    def consume(j, buf):
        jc = jnp.minimum(j, nb - 1)
        for h in range(hg):
            sj = jnp.where(sel_sc[h, pl.ds(jc, 1), :] > 0.0, buf[h], NEG_BIG)
            m_old = m_sc[h]
            m_new = jnp.maximum(m_old, jnp.max(sj, axis=0, keepdims=True))
            alpha = jnp.exp2(m_old - m_new)
            p = jnp.exp2(sj - m_new)
            acc_sc[h] = alpha * acc_sc[h] + jnp.dot(vt_sc[h, jc], p.astype(BF16),
                                                    preferred_element_type=F32)
            m_sc[h] = m_new

    stage_scores(0, s_sc)

    def past_pair(jj, carry):
        a = 2 * jj
        stage_scores(a + 1, s2_sc)
        consume(a, s_sc)
        stage_scores(a + 2, s_sc)
        consume(a + 1, s2_sc)
        return carry

    lax.fori_loop(0, (qi + 1) // 2, past_pair, 0)

    for h in range(hg):
        o = acc_sc[h, :dh, :] / acc_sc[h, dh:dh + 1, :]
        o_ref[:, cols(h)] = o.T.astype(o_ref.dtype)


def _attn_call(proj, rope_c, rope_s, B, S, H):
    T = B * S
    blk = MOBA_BLOCK
    nb = S // blk
    nbp = -(-nb // SUBLANES) * SUBLANES
    dh = HEAD_DIM
    hg = min(ATTN_HEAD_GROUP, H)
    ng = H // hg
    w = hg * dh
    kern = functools.partial(_attn_kernel, nb=nb, hg=hg)
    return pl.pallas_call(
        kern,
        grid=(B, ng, nb),
        in_specs=[
            pl.BlockSpec((blk, w), lambda b, g, i: (b * nb + i, g)),
            pl.BlockSpec((S, w), lambda b, g, i: (b, ng + g)),
            pl.BlockSpec((S, w), lambda b, g, i: (b, 2 * ng + g)),
            pl.BlockSpec((blk, dh), lambda b, g, i: (b * nb + i, 0)),
            pl.BlockSpec((blk, dh), lambda b, g, i: (b * nb + i, 0)),
            pl.BlockSpec((S, dh), lambda b, g, i: (b, 0)),
            pl.BlockSpec((S, dh), lambda b, g, i: (b, 0)),
        ],
        out_specs=pl.BlockSpec((blk, w), lambda b, g, i: (b * nb + i, g)),
        out_shape=jax.ShapeDtypeStruct((T, H * dh), BF16),
        scratch_shapes=[
            pltpu.VMEM((hg, nb, blk, dh), BF16),
            pltpu.VMEM((hg, nb, dh + SUM_ROWS, blk), BF16),
            pltpu.VMEM((hg, nbp, dh), F32),
            pltpu.VMEM((hg, nbp, blk), F32),
            pltpu.VMEM((hg, blk, dh), BF16),
            pltpu.VMEM((hg, blk, blk), F32),
            pltpu.VMEM((hg, blk, blk), F32),
            pltpu.VMEM((hg, 1, blk), F32),
            pltpu.VMEM((hg, dh + SUM_ROWS, blk), F32),
        ],
        compiler_params=_cparams(3),
        name="moba_attn",
    )(proj, proj, proj, rope_c, rope_s, rope_c, rope_s)


def _mix1_kernel(ya_ref, cb_ref, cc_ref, cx_ref, hcc_ref, hcx_ref, ga_ref, gc_ref, cw_ref,
                 wb0_ref, wb1_ref, o_ref, *, tm, S):
    i = pl.program_id(0)
    u = cc_ref[...].astype(F32) * cx_ref[...].astype(F32)
    hu = hcc_ref[...].astype(F32) * hcx_ref[...].astype(F32)
    seq_start = (i * tm) % S == 0
    hu = jnp.where(seq_start, 0.0, hu)
    h1 = hu[SUBLANES - 1:SUBLANES, :]
    h2 = hu[SUBLANES - 2:SUBLANES - 1, :]
    r = lax.broadcasted_iota(I32, u.shape, 0)
    u1 = jnp.where(r == 0, h1, pltpu.roll(u, 1, axis=0))
    u2 = jnp.where(r == 0, h2, jnp.where(r == 1, h1, pltpu.roll(u, 2, axis=0)))
    conv = cw_ref[0:1, :] * u2 + cw_ref[1:2, :] * u1 + cw_ref[2:3, :] * u
    yc = (cb_ref[...].astype(F32) * conv).astype(BF16)
    a = jnp.dot(ya_ref[...], wb0_ref[...], preferred_element_type=F32)
    c = jnp.dot(yc, wb1_ref[...], preferred_element_type=F32)
    merged = (jax.nn.sigmoid(ga_ref[...].astype(F32)) * a
              + jax.nn.sigmoid(gc_ref[...].astype(F32)) * c)
    o_ref[...] = merged.astype(o_ref.dtype)


def _mix1_call(ya, proj, conv_w, wb0, wb1, S, D):
    T, A = ya.shape
    assert conv_w.shape == (CONV_LEN, A) and (6 * A) % D == 0
    tm = _tile(S, 512, LANES)
    gcol = 6 * A // D
    hb = tm // SUBLANES
    kern = functools.partial(_mix1_kernel, tm=tm, S=S)
    halo = lambda col: pl.BlockSpec((SUBLANES, A), lambda i: (jnp.maximum(i * hb - 1, 0), col))
    return pl.pallas_call(
        kern,
        grid=(T // tm,),
        in_specs=[
            pl.BlockSpec((tm, A), lambda i: (i, 0)),
            pl.BlockSpec((tm, A), lambda i: (i, 3)),
            pl.BlockSpec((tm, A), lambda i: (i, 4)),
            pl.BlockSpec((tm, A), lambda i: (i, 5)),
            halo(4),
            halo(5),
            pl.BlockSpec((tm, D), lambda i: (i, gcol)),
            pl.BlockSpec((tm, D), lambda i: (i, gcol + 1)),
            pl.BlockSpec((CONV_LEN, A), lambda i: (0, 0)),
            pl.BlockSpec((A, D), lambda i: (0, 0)),
            pl.BlockSpec((A, D), lambda i: (0, 0)),
        ],
        out_specs=pl.BlockSpec((tm, D), lambda i: (i, 0)),
        out_shape=jax.ShapeDtypeStruct((T, D), BF16),
        compiler_params=_cparams(1),
        name="mix_merge",
    )(ya, proj, proj, proj, proj, proj, proj, proj, conv_w, wb0, wb1)


def _mix2_kernel(m_ref, wo_ref, x_ref, ada_ref, g_ref, wr_ref, xo_ref, hp_ref, lg_ref):
    out = jnp.dot(m_ref[...], wo_ref[...], preferred_element_type=F32)
    x = x_ref[...] + ada_ref[2:3, :] * out
    xo_ref[...] = x
    h = _rms_modulate(x, g_ref[...], ada_ref[3:4, :], ada_ref[4:5, :])
    tm, d = x.shape
    _store_rows(hp_ref, _pack_halves(h), tm, d // 2 // LANES)
    e = wr_ref.shape[1] // 2
    h_hi = h.astype(BF16)
    h_lo = (h - h_hi.astype(F32)).astype(BF16)
    lg2 = jnp.dot(h_hi, wr_ref[...], preferred_element_type=F32)
    lg1 = jnp.dot(h_lo, wr_ref[...], preferred_element_type=F32)
    lg = lg2[:, :e] + lg2[:, e:] + lg1[:, :e]
    lg_ref[...] = lg.T


def _mix2_call(merged, w_out, x, ada_l, g, wr, S):
    T, D = x.shape
    tm = _tile(S, 512, LANES)
    E = wr.shape[1] // 2
    ch = D // 2 // LANES
    return pl.pallas_call(
        _mix2_kernel,
        grid=(T // tm,),
        in_specs=[
            pl.BlockSpec((tm, D), lambda i: (i, 0)),
            pl.BlockSpec((D, D), lambda i: (0, 0)),
            pl.BlockSpec((tm, D), lambda i: (i, 0)),
            pl.BlockSpec((None, 6, D), lambda i: ((i * tm) // S, 0, 0)),
            pl.BlockSpec((1, D), lambda i: (0, 0)),
            pl.BlockSpec((D, 2 * E), lambda i: (0, 0)),
        ],
        out_specs=[
            pl.BlockSpec((tm, D), lambda i: (i, 0)),
            pl.BlockSpec((tm * ch, LANES), lambda i: (i, 0)),
            pl.BlockSpec((E, tm), lambda i: (0, i)),
        ],
        out_shape=[
            jax.ShapeDtypeStruct((T, D), F32),
            jax.ShapeDtypeStruct((T * ch, LANES), U32),
            jax.ShapeDtypeStruct((E, T), F32),
        ],
        compiler_params=_cparams(1),
        name="outproj_norm_router",
    )(merged, w_out, x, ada_l, g.reshape(1, D), wr)


def _route_kernel(lg_ref, bias_ref, tri_ref, idx_ref, wt_ref, rs_ref, cnt_ref, carry_sc):
    step = pl.program_id(0)

    @pl.when(step == 0)
    def _():
        carry_sc[...] = jnp.zeros_like(carry_sc)

    G = N_GROUPS
    scores = jax.nn.sigmoid(lg_ref[...])
    biased = scores + bias_ref[...]
    sc = [scores[i * G:(i + 1) * G, :] for i in range(GROUP_SIZE)]
    bi = [biased[i * G:(i + 1) * G, :] for i in range(GROUP_SIZE)]
    grow = lax.broadcasted_iota(I32, bi[0].shape, 0)
    eid = [grow * GROUP_SIZE + i for i in range(GROUP_SIZE)]
    neg_inf = jnp.float32(-jnp.inf)

    m1 = functools.reduce(jnp.maximum, bi)
    first = jnp.full(grow.shape, GROUP_SIZE, I32)
    for i in reversed(range(GROUP_SIZE)):
        first = jnp.where(bi[i] == m1, i, first)
    m2 = functools.reduce(jnp.maximum,
                          [jnp.where(first == i, neg_inf, bi[i]) for i in range(GROUP_SIZE)])
    gs = m1 + m2

    grank = jnp.zeros(gs.shape, F32)
    for gp in range(G):
        v = gs[gp:gp + 1, :]
        grank = grank + ((v > gs) | ((v == gs) & (gp < grow))).astype(F32)
    gsel = grank < TOPK_GROUPS

    ms = [jnp.where(gsel, b, neg_inf) for b in bi]
    chosen = [jnp.zeros(grow.shape, jnp.bool_) for _ in range(GROUP_SIZE)]
    picked_idx = []
    picked_score = []
    for _ in range(MOE_TOP_K):
        mx = jnp.max(functools.reduce(jnp.maximum, ms), axis=0, keepdims=True)
        cand = functools.reduce(
            jnp.minimum, [jnp.where(ms[i] == mx, eid[i], N_EXPERTS) for i in range(GROUP_SIZE)])
        ce = jnp.min(cand, axis=0, keepdims=True)
        hit = [eid[i] == ce for i in range(GROUP_SIZE)]
        s_k = jnp.sum(functools.reduce(
            jnp.add, [jnp.where(hit[i], sc[i], 0.0) for i in range(GROUP_SIZE)]),
            axis=0, keepdims=True)
        ms = [jnp.where(hit[i], neg_inf, ms[i]) for i in range(GROUP_SIZE)]
        chosen = [chosen[i] | hit[i] for i in range(GROUP_SIZE)]
        picked_idx.append(ce)
        picked_score.append(s_k)

    wsum = functools.reduce(jnp.add, picked_score)
    for k in range(MOE_TOP_K):
        idx_ref[k:k + 1, :] = picked_idx[k]
        wt_ref[k:k + 1, :] = picked_score[k] / wsum * ROUTED_SCALE

    selm = jnp.concatenate([c.astype(F32) for c in chosen], axis=0)
    pre = jnp.dot(selm.astype(BF16), tri_ref[...], preferred_element_type=F32) + carry_sc[:, 0:1]
    pre_slab = [pre[i * G:(i + 1) * G, :] for i in range(GROUP_SIZE)]
    for k in range(MOE_TOP_K):
        rk = jnp.sum(functools.reduce(
            jnp.add, [jnp.where(eid[i] == picked_idx[k], pre_slab[i], 0.0)
                      for i in range(GROUP_SIZE)]), axis=0, keepdims=True)
        rs_ref[k:k + 1, :] = rk.astype(I32)
    carry_sc[...] = carry_sc[...] + jnp.sum(selm, axis=1, keepdims=True)
    cnt_ref[...] = carry_sc[...]


def _route_call(lgt, bias_perm, T):
    E = lgt.shape[0]
    tr = _tile(T, 512, LANES)
    tri = (lax.broadcasted_iota(I32, (tr, tr), 0) < lax.broadcasted_iota(I32, (tr, tr), 1)).astype(BF16)
    return pl.pallas_call(
        _route_kernel,
        grid=(T // tr,),
        in_specs=[
            pl.BlockSpec((E, tr), lambda i: (0, i)),
            pl.BlockSpec((E, 1), lambda i: (0, 0)),
            pl.BlockSpec((tr, tr), lambda i: (0, 0)),
        ],
        out_specs=[
            pl.BlockSpec((MOE_TOP_K, tr), lambda i: (0, i)),
            pl.BlockSpec((MOE_TOP_K, tr), lambda i: (0, i)),
            pl.BlockSpec((MOE_TOP_K, tr), lambda i: (0, i)),
            pl.BlockSpec((E, 128), lambda i: (0, 0)),
        ],
        out_shape=[
            jax.ShapeDtypeStruct((MOE_TOP_K, T), I32),
            jax.ShapeDtypeStruct((MOE_TOP_K, T), F32),
            jax.ShapeDtypeStruct((MOE_TOP_K, T), I32),
            jax.ShapeDtypeStruct((E, 128), F32),
        ],
        scratch_shapes=[pltpu.VMEM((E, 128), F32)],
        compiler_params=_cparams(1),
        name="route",
    )(lgt, bias_perm.reshape(E, 1), tri)


def _dispatch_kernel(zs_ref, nu_ref, hp_ref, dest_ref, xs_ref, z_sc, sem, *, td, ch, n_blocks):
    i = pl.program_id(0)
    zrows = EXPERT_ROWS * ch

    @pl.when(i == 0)
    def _():
        z_sc[...] = jnp.zeros_like(z_sc)
        zero_wait = pltpu.make_async_copy(z_sc, xs_ref.at[pl.ds(0, zrows), :], sem)

        def zero_block(b, carry):
            b0 = pl.multiple_of(b * zrows, zrows)
            pltpu.make_async_copy(z_sc, xs_ref.at[pl.ds(b0, zrows), :], sem).start()
            return carry

        def wait_block(b, carry):
            zero_wait.wait()
            return carry

        for e in range(N_EXPERTS):
            @pl.when(zs_ref[e] >= 0)
            def _(e=e):
                zero_block(zs_ref[e], 0)
        lax.fori_loop(nu_ref[0], n_blocks, zero_block, 0)
        for e in range(N_EXPERTS):
            @pl.when(zs_ref[e] >= 0)
            def _():
                zero_wait.wait()
        lax.fori_loop(nu_ref[0], n_blocks, wait_block, 0)

    def issue(tp, carry):
        for u in range(ISSUE_UNROLL):
            t = tp * ISSUE_UNROLL + u
            src = hp_ref.at[pl.ds(pl.multiple_of(t * ch, ch), ch), :]
            for k in range(MOE_TOP_K):
                r0 = pl.multiple_of(dest_ref[t * MOE_TOP_K + k] * ch, ch)
                pltpu.make_async_copy(src, xs_ref.at[pl.ds(r0, ch), :], sem).start(priority=k % 2)
        return carry

    lax.fori_loop(0, td // ISSUE_UNROLL, issue, 0)
    for k in range(MOE_TOP_K):
        pltpu.make_async_copy(hp_ref, xs_ref.at[pl.ds(0, td * ch), :], sem).wait()


def _dispatch_call(hp, dest_flat, zstart, n_used, n_blocks, ch):
    T = hp.shape[0] // ch
    td = _tile(T, 256, SUBLANES * ISSUE_UNROLL)
    n_rows = n_blocks * EXPERT_ROWS
    kern = functools.partial(_dispatch_kernel, td=td, ch=ch, n_blocks=n_blocks)
    return pl.pallas_call(
        kern,
        grid_spec=pltpu.PrefetchScalarGridSpec(
            num_scalar_prefetch=2,
            grid=(T // td,),
            in_specs=[
                pl.BlockSpec((td * ch, LANES), lambda i, zs, nu: (i, 0)),
                pl.BlockSpec((td * MOE_TOP_K,), lambda i, zs, nu: (i,), memory_space=pltpu.SMEM),
            ],
            out_specs=pl.BlockSpec(memory_space=pl.ANY),
            scratch_shapes=[pltpu.VMEM((EXPERT_ROWS * ch, LANES), U32), pltpu.SemaphoreType.DMA],
        ),
        out_shape=jax.ShapeDtypeStruct((n_rows * ch, LANES), U32),
        compiler_params=_cparams(1),
        name="dispatch",
    )(zstart, n_used, hp, dest_flat)


def _expert_kernel(be_ref, nu_ref, nx_ref, sl_ref, ve_ref, xs_ref, wg_hbm, wu_hbm, wd_hbm, o_ref,
                   wg_st, wu_st, wd_st, wg_sc, wu_sc, wd_sc, sems, *, layer):
    i = pl.program_id(0)
    e = be_ref[i]
    prev = be_ref[jnp.maximum(i - 1, 0)]
    changed = (i == 0) | (e != prev)

    def weight_copies(ex, slot):
        return (pltpu.make_async_copy(wg_hbm.at[layer, ex], wg_st.at[slot], sems.at[slot, 0]),
                pltpu.make_async_copy(wu_hbm.at[layer, ex], wu_st.at[slot], sems.at[slot, 1]),
                pltpu.make_async_copy(wd_hbm.at[layer, ex], wd_st.at[slot], sems.at[slot, 2]))

    @pl.when(i == 0)
    def _():
        for cp in weight_copies(e, sl_ref[e]):
            cp.start(priority=1)

    @pl.when(changed)
    def _():
        slot = sl_ref[e]
        for cp in weight_copies(e, slot):
            cp.wait()
        wg_sc[...] = wg_st[slot].astype(BF16)
        wu_sc[...] = wu_st[slot].astype(BF16)
        wd_sc[...] = wd_st[slot].astype(BF16)
        nxt = nx_ref[e]

        @pl.when(nxt >= 0)
        def _():
            for cp in weight_copies(nxt, 1 - slot):
                cp.start(priority=1)

    ch = wg_sc.shape[0] // 2 // LANES
    head = EXPERT_ROWS // 2
    valid = ve_ref[e] - i * EXPERT_ROWS

    def ffn(n_rows):
        rows = pl.ds(0, n_rows * ch)
        w = _load_rows(xs_ref.at[rows, :], n_rows, ch)
        x = jnp.concatenate([_unpack_lo(w).astype(BF16), _unpack_hi(w).astype(BF16)], axis=1)
        gate = jnp.dot(x, wg_sc[...], preferred_element_type=F32)
        up = jnp.dot(x, wu_sc[...], preferred_element_type=F32)
        act = (_silu(gate) * up).astype(BF16)
        out = jnp.dot(act, wd_sc[...], preferred_element_type=F32)
        _store_rows(o_ref.at[rows, :], _pack_halves(out), n_rows, ch)

    @pl.when(valid > head)
    def _():
        ffn(EXPERT_ROWS)

    @pl.when((valid > 0) & (valid <= head))
    def _():
        ffn(head)
        o_ref[pl.ds(head * ch, head * ch), :] = jnp.zeros((head * ch, LANES), U32)

    @pl.when(valid <= 0)
    def _():
        o_ref[...] = jnp.zeros_like(o_ref)


def _expert_call(xs, block_e, n_used, next_e, slot_e, seg_valid_end, w_gate, w_up, w_down, layer,
                 n_blocks):
    _, E, D, F = w_gate.shape
    ch = D // 2 // LANES
    rows = lambda i, be, nu, nx, sl, ve: (jnp.minimum(i, nu[0] - 1), 0)
    out_rows = lambda i, be, nu, nx, sl, ve: (i, 0)
    kern = functools.partial(_expert_kernel, layer=layer)
    return pl.pallas_call(
        kern,
        grid_spec=pltpu.PrefetchScalarGridSpec(
            num_scalar_prefetch=5,
            grid=(n_blocks,),
            in_specs=[
                pl.BlockSpec((EXPERT_ROWS * ch, LANES), rows),
                pl.BlockSpec(memory_space=pl.ANY),
                pl.BlockSpec(memory_space=pl.ANY),
                pl.BlockSpec(memory_space=pl.ANY),
            ],
            out_specs=pl.BlockSpec((EXPERT_ROWS * ch, LANES), out_rows),
            scratch_shapes=[
                pltpu.VMEM((2, D, F), F32), pltpu.VMEM((2, D, F), F32), pltpu.VMEM((2, F, D), F32),
                pltpu.VMEM((D, F), BF16), pltpu.VMEM((D, F), BF16), pltpu.VMEM((F, D), BF16),
                pltpu.SemaphoreType.DMA((2, 3)),
            ],
        ),
        out_shape=jax.ShapeDtypeStruct((n_blocks * EXPERT_ROWS * ch, LANES), U32),
        compiler_params=_cparams(1),
        name="experts",
    )(block_e, n_used, next_e, slot_e, seg_valid_end, xs, w_gate, w_up, w_down)


def _combine_kernel(dest_ref, wt_ref, hp_ref, x_ref, ada_ref, wsg_ref, wsu_ref, wsd_ref, fn_ref,
                    eo_ref, xo_ref, gbuf, sem, *, tc, final):
    half = x_ref.shape[1] // 2
    ch = half // LANES

    def issue(tp, carry):
        for u in range(ISSUE_UNROLL):
            t = tp * ISSUE_UNROLL + u
            t0 = pl.multiple_of(t * ch, ch)
            for k in range(MOE_TOP_K):
                r0 = pl.multiple_of(dest_ref[t * MOE_TOP_K + k] * ch, ch)
                pltpu.make_async_copy(eo_ref.at[pl.ds(r0, ch), :], gbuf.at[k, pl.ds(t0, ch), :],
                                      sem).start(priority=k % 2)
        return carry

    lax.fori_loop(0, tc // ISSUE_UNROLL, issue, 0)

    w = _load_rows(hp_ref, tc, ch)
    h = jnp.concatenate([_unpack_lo(w).astype(BF16), _unpack_hi(w).astype(BF16)], axis=1)
    gate = jnp.dot(h, wsg_ref[...], preferred_element_type=F32)
    up = jnp.dot(h, wsu_ref[...], preferred_element_type=F32)
    act = (_silu(gate) * up).astype(BF16)
    shared = jnp.dot(act, wsd_ref[...], preferred_element_type=F32)

    for k in range(MOE_TOP_K):
        pltpu.make_async_copy(eo_ref.at[pl.ds(0, tc * ch), :], gbuf.at[k], sem).wait()

    wts = wt_ref[...]
    ylo = shared[:, :half]
    yhi = shared[:, half:]
    for k in range(MOE_TOP_K):
        g = _load_rows(gbuf.at[k], tc, ch)
        wk = wts[:, k:k + 1]
        ylo = ylo + wk * _unpack_lo(g)
        yhi = yhi + wk * _unpack_hi(g)
    gf = ada_ref[5:6, :]
    xlo = x_ref[:, :half] + gf[:, :half] * ylo
    xhi = x_ref[:, half:] + gf[:, half:] * yhi
    if final:
        ss = (jnp.sum(xlo * xlo, axis=-1, keepdims=True)
              + jnp.sum(xhi * xhi, axis=-1, keepdims=True))
        inv = lax.rsqrt(ss / (2 * half) + NORM_EPS)
        xlo = xlo * inv * fn_ref[:, :half]
        xhi = xhi * inv * fn_ref[:, half:]
    xo_ref[:, :half] = xlo
    xo_ref[:, half:] = xhi


def _combine_call(dest_flat, wts, hp, x, ada_l, wsg, wsu, wsd, fnorm, eo, S, final):
    T, D = x.shape
    ch = D // 2 // LANES
    Fs = wsg.shape[1]
    tc = _tile(S, 256, SUBLANES * ISSUE_UNROLL)
    kern = functools.partial(_combine_kernel, tc=tc, final=final)
    return pl.pallas_call(
        kern,
        grid=(T // tc,),
        in_specs=[
            pl.BlockSpec((tc * MOE_TOP_K,), lambda i: (i,), memory_space=pltpu.SMEM),
            pl.BlockSpec((tc, MOE_TOP_K), lambda i: (i, 0)),
            pl.BlockSpec((tc * ch, LANES), lambda i: (i, 0)),
            pl.BlockSpec((tc, D), lambda i: (i, 0)),
            pl.BlockSpec((None, 6, D), lambda i: ((i * tc) // S, 0, 0)),
            pl.BlockSpec((D, Fs), lambda i: (0, 0)),
            pl.BlockSpec((D, Fs), lambda i: (0, 0)),
            pl.BlockSpec((Fs, D), lambda i: (0, 0)),
            pl.BlockSpec((1, D), lambda i: (0, 0)),
            pl.BlockSpec(memory_space=pl.ANY),
        ],
        out_specs=pl.BlockSpec((tc, D), lambda i: (i, 0)),
        out_shape=jax.ShapeDtypeStruct((T, D), F32),
        scratch_shapes=[pltpu.VMEM((MOE_TOP_K, tc * ch, LANES), U32), pltpu.SemaphoreType.DMA],
        compiler_params=_cparams(1),
        name="combine_shared",
    )(dest_flat, wts, hp, x, ada_l, wsg, wsu, wsd, fnorm.reshape(1, D), eo)


def _rope_tables(positions):
    half = ROPE_DIM // 2
    inv_freq = ROPE_THETA ** (-jnp.arange(0, ROPE_DIM, 2, dtype=F32) / ROPE_DIM)
    ang = positions.astype(F32).reshape(-1, 1) * inv_freq
    cos, sin = jnp.cos(ang), jnp.sin(ang)
    T = ang.shape[0]
    rest = HEAD_DIM - ROPE_DIM
    c = jnp.concatenate([cos, cos, jnp.ones((T, rest), F32)], axis=-1)
    s = jnp.concatenate([-sin, sin, jnp.zeros((T, rest), F32)], axis=-1)
    return c, s


def _expert_row_perm():
    r = jnp.arange(N_EXPERTS)
    return (r % N_GROUPS) * GROUP_SIZE + r // N_GROUPS


def kernel(x, c, positions, norm_mix, w_ada, b_ada, w_in, conv_w, w_branch, w_out, norm_ffn, w_router, router_bias, w_gate, w_up, w_down, ws_gate, ws_up, ws_down, final_norm):
    B, S, D = x.shape
    L = w_ada.shape[0]
    T = B * S
    A = w_branch.shape[2]
    H = A // HEAD_DIM
    assert S % MOBA_BLOCK == 0 and w_in.shape[2] == 6 * A + 2 * D

    n_assign = T * MOE_TOP_K
    n_blocks = -(-(n_assign + N_EXPERTS * (EXPERT_ROWS - 1)) // EXPERT_ROWS)

    rope_c, rope_s = _rope_tables(positions)
    ada = _ada_call(c, w_ada, b_ada).reshape(L, B, 6, D)
    perm = _expert_row_perm()
    xf = x.reshape(T, D)

    for l in range(L):
        proj = _inproj_call(xf, norm_mix[l], ada[l], w_in[l].astype(BF16), S)
        ya = _attn_call(proj, rope_c, rope_s, B, S, H)
        merged = _mix1_call(ya, proj, conv_w[l], w_branch[l, 0].astype(BF16),
                            w_branch[l, 1].astype(BF16), S, D)
        wr = w_router[l][:, perm]
        wr_hi = wr.astype(BF16)
        wr_lo = (wr - wr_hi.astype(F32)).astype(BF16)
        xf, hp, lgt = _mix2_call(merged, w_out[l].astype(BF16), xf, ada[l], norm_ffn[l],
                                 jnp.concatenate([wr_hi, wr_lo], axis=1), S)

        idx_t, wt_t, rs_t, cnt = _route_call(lgt, router_bias[l][perm], T)
        counts = cnt[perm, 0].astype(I32)
        padded = (counts + EXPERT_ROWS - 1) // EXPERT_ROWS * EXPERT_ROWS
        pad_ends = jnp.cumsum(padded)
        pad_starts = pad_ends - padded
        seg_start = jnp.sum(jnp.where(idx_t[:, :, None] == jnp.arange(N_EXPERTS), pad_starts, 0),
                            axis=-1)
        dest_flat = (seg_start + rs_t).T.reshape(-1)
        n_used = (pad_ends[-1] // EXPERT_ROWS).astype(I32)
        blk = jnp.minimum(jnp.arange(n_blocks, dtype=I32), n_used - 1)
        block_e = jnp.minimum(
            jnp.sum(pad_ends[None, :] <= (blk * EXPERT_ROWS)[:, None], axis=1),
            N_EXPERTS - 1).astype(I32)

        n_used = n_used.reshape(1)
        last_blk = jnp.where(padded > 0, pad_ends // EXPERT_ROWS - 1, -1).astype(I32)
        xs = _dispatch_call(hp, dest_flat, last_blk, n_used, n_blocks, D // 2 // LANES)
        present = padded > 0
        eids = jnp.arange(N_EXPERTS, dtype=I32)
        suffix_min = lax.cummin(jnp.where(present, eids, N_EXPERTS)[::-1])[::-1]
        next_e = jnp.concatenate([suffix_min[1:], jnp.full((1,), N_EXPERTS, I32)])
        next_e = jnp.where(next_e >= N_EXPERTS, -1, next_e).astype(I32)
        slot_e = ((jnp.cumsum(present) - present) % 2).astype(I32)
        seg_valid_end = (pad_starts + counts).astype(I32)
        eo = _expert_call(xs, block_e, n_used, next_e, slot_e, seg_valid_end, w_gate, w_up, w_down,
                          l, n_blocks)
        xf = _combine_call(dest_flat, wt_t.T, hp, xf, ada[l], ws_gate[l].astype(BF16),
                           ws_up[l].astype(BF16), ws_down[l].astype(BF16), final_norm, eo, S,
                           final=(l == L - 1))
    return xf.reshape(B, S, D)
```
